```python
import math
import jax, jax.numpy as jnp
from jax import lax
import numpy as np

D_MODEL = 4096
BATCH = 2
SEQ = 4096
DEPTH = 2

HEAD_DIM = 128
MIX_WIDTH = D_MODEL
GDN_HEADS = 3 * MIX_WIDTH // (8 * HEAD_DIM)
GDN_WIDTH = GDN_HEADS * HEAD_DIM
CONV_WIDTH = 4
GDN_CHUNK = 64
HGRN_HEADS = MIX_WIDTH // (4 * HEAD_DIM)
HGRN_WIDTH = HGRN_HEADS * HEAD_DIM
HGRN_CHUNK = 64
DIL_PAIRS = ((128, 1), (512, 4), (2048, 16))
DIL_GROUPS = 3
DIL_KV_HEADS = (MIX_WIDTH - GDN_WIDTH - HGRN_WIDTH) // (DIL_GROUPS * HEAD_DIM)
DIL_Q_HEADS = DIL_GROUPS * DIL_KV_HEADS
DIL_Q_WIDTH = DIL_Q_HEADS * HEAD_DIM
DIL_KV_WIDTH = DIL_KV_HEADS * HEAD_DIM
NUM_BUCKETS = 32
MAX_DISTANCE = 2048
IN_SIZES = (3 * GDN_WIDTH, GDN_WIDTH, GDN_HEADS, GDN_HEADS,
            HGRN_WIDTH, HGRN_WIDTH, HGRN_WIDTH, HGRN_WIDTH,
            DIL_Q_WIDTH, DIL_KV_WIDTH, DIL_KV_WIDTH)
IN_WIDTH = 4 * GDN_WIDTH + 2 * GDN_HEADS + 4 * HGRN_WIDTH + DIL_Q_WIDTH + 2 * DIL_KV_WIDTH
OUT_WIDTH = GDN_WIDTH + HGRN_WIDTH + DIL_KV_WIDTH
D_FF_DENSE = 7 * D_MODEL // 2
N_EXPERTS = 8
TOP_K = 2
D_FF_EXPERT = 11 * D_MODEL // 8
N_DENSE = (DEPTH + 1) // 2
N_MOE = DEPTH // 2
RMS_EPS = 1e-6
MASK_VALUE = -1e30
MIN_GATE = 1e-20

kernel_name = "hybrid_gdn_hgrn2_dilated_moe_block"


def rmsnorm(x, gain):
    xf = x.astype(jnp.float32)
    y = xf * lax.rsqrt(jnp.mean(xf * xf, axis=-1, keepdims=True) + RMS_EPS)
    return (y * gain.astype(jnp.float32)).astype(x.dtype)


def l2norm(x):
    return x * lax.rsqrt(jnp.sum(x * x, axis=-1, keepdims=True) + RMS_EPS)


def causal_short_conv(x, w):
    k_len = w.shape[0]
    s_len = x.shape[1]
    xp = jnp.pad(x, ((0, 0), (k_len - 1, 0), (0, 0)))
    y = xp[:, 0:s_len] * w[0]
    for j in range(1, k_len):
        y = y + xp[:, j:j + s_len] * w[j]
    return jax.nn.silu(y)


def masked_exp(diff, mask):
    return jnp.where(mask, jnp.exp(jnp.where(mask, diff, 0.0)), 0.0)


def gated_delta_rule_chunked(q, k, v, g, beta):
    b_sz, s_len, n_h, d_k = q.shape
    c = GDN_CHUNK
    n_c = s_len // c

    def chunk(t):
        return t.reshape(b_sz, n_c, c, n_h, -1).transpose(1, 0, 3, 2, 4)

    q = chunk(q) * (d_k ** -0.5)
    k = chunk(k)
    v = chunk(v)
    g = g.reshape(b_sz, n_c, c, n_h).transpose(1, 0, 3, 2)
    beta = beta.reshape(b_sz, n_c, c, n_h).transpose(1, 0, 3, 2)
    gc = jnp.cumsum(g, axis=-1)
    causal = jnp.tril(jnp.ones((c, c), bool))
    strict = jnp.tril(jnp.ones((c, c), bool), -1)
    decay = masked_exp(gc[..., :, None] - gc[..., None, :], causal)
    k_beta = k * beta[..., None]
    lower = jnp.where(strict, jnp.einsum('nbhid,nbhjd->nbhij', k_beta, k) * decay, 0.0)
    t_mat = lower + jnp.eye(c, dtype=q.dtype)
    u = lax.linalg.triangular_solve(t_mat, v * beta[..., None], left_side=True, lower=True,
                                    unit_diagonal=True)
    w = lax.linalg.triangular_solve(t_mat, k_beta * jnp.exp(gc)[..., None], left_side=True,
                                    lower=True, unit_diagonal=True)
    attn = jnp.einsum('nbhid,nbhjd->nbhij', q, k) * decay
    q_dec = q * jnp.exp(gc)[..., None]
    g_last = gc[..., -1]
    k_dec = k * jnp.exp(g_last[..., None] - gc)[..., None]

    def step(state, inp):
        q_dec_c, k_dec_c, u_c, w_c, attn_c, gl = inp
        v_new = u_c - jnp.einsum('bhck,bhkv->bhcv', w_c, state)
        o = (jnp.einsum('bhck,bhkv->bhcv', q_dec_c, state)
             + jnp.einsum('bhij,bhjv->bhiv', attn_c, v_new))
        state = (state * jnp.exp(gl)[..., None, None]
                 + jnp.einsum('bhck,bhcv->bhkv', k_dec_c, v_new))
        return state, o

    state0 = jnp.zeros((b_sz, n_h, d_k, v.shape[-1]), q.dtype)
    _, o = lax.scan(step, state0, (q_dec, k_dec, u, w, attn, g_last))
    return o.transpose(1, 0, 3, 2, 4).reshape(b_sz, s_len, n_h, -1)


def hgrn2_chunked(q, k, v, log_f):
    b_sz, s_len, n_h, d_k = q.shape
    c = HGRN_CHUNK
    n_c = s_len // c

    def chunk(t):
        return t.reshape(b_sz, n_c, c, n_h, -1).transpose(1, 0, 3, 2, 4)

    q, k, v = chunk(q), chunk(k), chunk(v)
    b = jnp.cumsum(chunk(log_f), axis=-2)
    b_last = b[..., -1, :]
    q_dec = q * jnp.exp(b)
    k_dec = k * jnp.exp(b_last[..., None, :] - b)
    causal = jnp.tril(jnp.ones((c, c), bool))[:, :, None]

    def step(state, inp):
        q_c, k_c, v_c, b_c, q_dec_c, k_dec_c, bl = inp
        diff = b_c[:, :, :, None, :] - b_c[:, :, None, :, :]
        dec = masked_exp(diff, causal)
        a = jnp.einsum('bhtk,bhsk,bhtsk->bhts', q_c, k_c, dec)
        o = (jnp.einsum('bhts,bhsv->bhtv', a, v_c)
             + jnp.einsum('bhtk,bhkv->bhtv', q_dec_c, state))
        state = state * jnp.exp(bl)[..., None] + jnp.einsum('bhsk,bhsv->bhkv', k_dec_c, v_c)
        return state, o

    state0 = jnp.zeros((b_sz, n_h, d_k, v.shape[-1]), q.dtype)
    _, o = lax.scan(step, state0, (q, k, v, b, q_dec, k_dec, b_last))
    return o.transpose(1, 0, 3, 2, 4).reshape(b_sz, s_len, n_h, -1)


def t5_bucket(dist):
    max_exact = NUM_BUCKETS // 2
    d = jnp.maximum(dist, 1).astype(jnp.float32)
    large = max_exact + (jnp.log(d / max_exact) / math.log(MAX_DISTANCE / max_exact)
                         * (NUM_BUCKETS - max_exact)).astype(jnp.int32)
    large = jnp.clip(large, 0, NUM_BUCKETS - 1)
    return jnp.where(dist < max_exact, dist, large)


def dilated_window_attention(q, k, v, rel_bias, window, dilation):
    b_sz, s_len, n_h, d_h = q.shape
    blk = window // dilation
    n = s_len // dilation
    nb = -(-n // blk)
    n_pad = nb * blk

    def strided(t):
        return (t.reshape(b_sz, n, dilation, n_h, -1).transpose(0, 2, 1, 3, 4)
                .reshape(b_sz * dilation, n, n_h, -1))

    def kv_blocks(t):
        tp = jnp.pad(t, ((0, 0), (blk, n_pad - n), (0, 0), (0, 0)))
        prev = tp[:, :n_pad].reshape(-1, nb, blk, n_h, t.shape[-1])
        cur = tp[:, blk:].reshape(-1, nb, blk, n_h, t.shape[-1])
        return jnp.concatenate([prev, cur], axis=2)

    qb = jnp.pad(strided(q), ((0, 0), (0, n_pad - n), (0, 0), (0, 0))).reshape(-1, nb, blk, n_h, d_h)
    kb = kv_blocks(strided(k))
    vb = kv_blocks(strided(v))

    qi = jnp.arange(blk)[:, None]
    ki = jnp.arange(2 * blk)[None, :]
    steps = qi + blk - ki
    in_window = (steps >= 0) & (steps <= blk)
    bias = rel_bias.astype(jnp.float32)[t5_bucket(jnp.clip(steps, 0, blk) * dilation)]
    bias = bias.transpose(2, 0, 1)
    first_block = (jnp.arange(nb) == 0)[:, None, None]
    valid = in_window[None] & ~(first_block & (ki < blk)[None])

    logits = (jnp.einsum('bnqhd,bnkhd->bnhqk', qb, kb).astype(jnp.float32) * (d_h ** -0.5)
              + bias[None, None])
    logits = jnp.where(valid[None, :, None], logits, MASK_VALUE)
    mx = jnp.max(logits, axis=-1, keepdims=True)
    p = jnp.exp(logits - mx)
    den = jnp.sum(p, axis=-1, keepdims=True)
    o = jnp.einsum('bnhqk,bnkhd->bnqhd', p / den, vb.astype(jnp.float32))
    log_den = (mx + jnp.log(den))[..., 0]

    def unstride(t):
        t = t[:, :n]
        t = t.reshape((b_sz, dilation, n) + t.shape[2:])
        t = jnp.swapaxes(t, 1, 2)
        return t.reshape((b_sz, s_len) + t.shape[3:])

    o = unstride(o.reshape(-1, n_pad, n_h, d_h))
    log_den = unstride(log_den.transpose(0, 1, 3, 2).reshape(-1, n_pad, n_h))
    return o, log_den


def hybrid_mixer(h, w_in, conv_w, a_log, dt_bias, gdn_gain, lower_bound, hgrn_gain, rel_bias, w_out):
    f32 = jnp.float32
    b_sz, s_len, _ = h.shape
    proj = h @ w_in
    points = np.cumsum(IN_SIZES)[:-1].tolist()
    (gdn_qkv, gdn_z, gdn_b, gdn_a, hg_q, hg_f, hg_i, hg_g,
     dil_q, dil_k, dil_v) = jnp.split(proj, points, axis=-1)

    def heads(t, n_h):
        return t.reshape(b_sz, s_len, n_h, -1).astype(f32)

    qkv = causal_short_conv(gdn_qkv.astype(f32), conv_w.astype(f32))
    qa, ka, va = jnp.split(qkv, 3, axis=-1)
    qa = l2norm(heads(qa, GDN_HEADS))
    ka = l2norm(heads(ka, GDN_HEADS))
    va = heads(va, GDN_HEADS)
    beta = jax.nn.sigmoid(gdn_b.astype(f32))
    g = -jnp.exp(a_log.astype(f32)) * jax.nn.softplus(gdn_a.astype(f32) + dt_bias.astype(f32))
    oa = gated_delta_rule_chunked(qa, ka, va, g, beta)
    oa = rmsnorm(oa, gdn_gain) * jax.nn.silu(heads(gdn_z, GDN_HEADS))

    lb = lower_bound.reshape(HGRN_HEADS, HEAD_DIM)
    f_pre = heads(hg_f, HGRN_HEADS)
    f_gate = lb + (1.0 - lb) * jax.nn.sigmoid(f_pre)
    log_f = jnp.log(jnp.maximum(f_gate, MIN_GATE))
    k_in = (1.0 - lb) * jax.nn.sigmoid(-f_pre)
    ob = hgrn2_chunked(jax.nn.silu(heads(hg_q, HGRN_HEADS)), k_in, heads(hg_i, HGRN_HEADS), log_f)
    ob = rmsnorm(ob, hgrn_gain) * jax.nn.silu(heads(hg_g, HGRN_HEADS))

    qc = dil_q.reshape(b_sz, s_len, DIL_Q_HEADS, HEAD_DIM)
    kc = dil_k.reshape(b_sz, s_len, DIL_KV_HEADS, HEAD_DIM)
    vc = dil_v.reshape(b_sz, s_len, DIL_KV_HEADS, HEAD_DIM)
    outs, dens = [], []
    for gi, (window, dilation) in enumerate(DIL_PAIRS):
        sl = slice(gi * DIL_KV_HEADS, (gi + 1) * DIL_KV_HEADS)
        o_g, d_g = dilated_window_attention(qc[:, :, sl], kc, vc, rel_bias[:, sl], window, dilation)
        outs.append(o_g)
        dens.append(d_g)
    mix_w = jax.nn.softmax(jnp.stack(dens, axis=0), axis=0)
    oc = jnp.sum(mix_w[..., None] * jnp.stack(outs, axis=0), axis=0)

    mixed = jnp.concatenate([oa.reshape(b_sz, s_len, -1), ob.reshape(b_sz, s_len, -1),
                             oc.reshape(b_sz, s_len, -1)], axis=-1).astype(h.dtype)
    return mixed @ w_out


def swiglu(h, w_gate, w_up, w_down):
    return (jax.nn.silu(h @ w_gate) * (h @ w_up)) @ w_down


def moe_swiglu(h, w_router, w_gate, w_up, w_down):
    b_sz, s_len, d = h.shape
    t = h.reshape(-1, d)
    logits = (t @ w_router).astype(jnp.float32)
    top_val, top_idx = lax.top_k(logits, TOP_K)
    top_w = jax.nn.softmax(top_val, axis=-1)
    gates = jnp.sum(jax.nn.one_hot(top_idx, N_EXPERTS, dtype=jnp.float32) * top_w[..., None], axis=1)
    y = jnp.zeros_like(t)
    for e in range(N_EXPERTS):
        y = y + gates[:, e:e + 1].astype(t.dtype) * swiglu(t, w_gate[e], w_up[e], w_down[e])
    return y.reshape(b_sz, s_len, d)


def setup_inputs(seed: int = 0) -> dict:
    key = jax.random.key(seed)
    ks = jax.random.split(key, 20)
    f32 = jnp.float32

    def nrm(k, shape, scale):
        return jax.random.normal(k, shape, f32) * scale

    x = nrm(ks[0], (BATCH, SEQ, D_MODEL), 1.0)
    w_in = nrm(ks[1], (DEPTH, D_MODEL, IN_WIDTH), D_MODEL ** -0.5)
    conv_gdn = nrm(ks[2], (DEPTH, CONV_WIDTH, 3 * GDN_WIDTH), CONV_WIDTH ** -0.5)
    gdn_a_log = jnp.log(jax.random.uniform(ks[3], (DEPTH, GDN_HEADS), f32, 1.0, 16.0))
    dt = jnp.exp(jax.random.uniform(ks[4], (DEPTH, GDN_HEADS), f32, math.log(1e-3), math.log(1e-1)))
    gdn_dt_bias = dt + jnp.log(-jnp.expm1(-dt))
    gdn_norm = 1.0 + nrm(ks[5], (DEPTH, HEAD_DIM), 0.02)
    hgrn_lb = nrm(ks[6], (DEPTH, HGRN_WIDTH), 0.5)
    hgrn_norm = 1.0 + nrm(ks[7], (DEPTH, HEAD_DIM), 0.02)
    rel_bias = nrm(ks[8], (NUM_BUCKETS, DIL_Q_HEADS), 0.5)
    w_out = nrm(ks[9], (DEPTH, OUT_WIDTH, D_MODEL), OUT_WIDTH ** -0.5)
    norm_mix = 1.0 + nrm(ks[10], (DEPTH, D_MODEL), 0.02)
    norm_ffn = 1.0 + nrm(ks[11], (DEPTH, D_MODEL), 0.02)
    w_gate_dense = nrm(ks[12], (N_DENSE, D_MODEL, D_FF_DENSE), D_MODEL ** -0.5)
    w_up_dense = nrm(ks[13], (N_DENSE, D_MODEL, D_FF_DENSE), D_MODEL ** -0.5)
    w_down_dense = nrm(ks[14], (N_DENSE, D_FF_DENSE, D_MODEL), D_FF_DENSE ** -0.5)
    w_router = nrm(ks[15], (N_MOE, D_MODEL, N_EXPERTS), D_MODEL ** -0.5)
    w_gate_moe = nrm(ks[16], (N_MOE, N_EXPERTS, D_MODEL, D_FF_EXPERT), D_MODEL ** -0.5)
    w_up_moe = nrm(ks[17], (N_MOE, N_EXPERTS, D_MODEL, D_FF_EXPERT), D_MODEL ** -0.5)
    w_down_moe = nrm(ks[18], (N_MOE, N_EXPERTS, D_FF_EXPERT, D_MODEL), D_FF_EXPERT ** -0.5)
    norm_final = 1.0 + nrm(ks[19], (D_MODEL,), 0.02)
    return {"x": x, "w_in": w_in, "conv_gdn": conv_gdn, "gdn_a_log": gdn_a_log,
            "gdn_dt_bias": gdn_dt_bias, "gdn_norm": gdn_norm, "hgrn_lb": hgrn_lb,
            "hgrn_norm": hgrn_norm, "rel_bias": rel_bias, "w_out": w_out, "norm_mix": norm_mix,
            "norm_ffn": norm_ffn, "w_gate_dense": w_gate_dense, "w_up_dense": w_up_dense,
            "w_down_dense": w_down_dense, "w_router": w_router, "w_gate_moe": w_gate_moe,
            "w_up_moe": w_up_moe, "w_down_moe": w_down_moe, "norm_final": norm_final}


def reference(x, w_in, conv_gdn, gdn_a_log, gdn_dt_bias, gdn_norm, hgrn_lb, hgrn_norm, rel_bias,
              w_out, norm_mix, norm_ffn, w_gate_dense, w_up_dense, w_down_dense, w_router,
              w_gate_moe, w_up_moe, w_down_moe, norm_final):
    probs = jax.nn.softmax(hgrn_lb.astype(jnp.float32), axis=0)
    lower_bounds = jnp.cumsum(probs, axis=0) - probs[0]
    h = x
    for layer in range(DEPTH):
        hn = rmsnorm(h, norm_mix[layer])
        h = h + hybrid_mixer(hn, w_in[layer], conv_gdn[layer], gdn_a_log[layer], gdn_dt_bias[layer],
                             gdn_norm[layer], lower_bounds[layer], hgrn_norm[layer], rel_bias,
                             w_out[layer])
        hn = rmsnorm(h, norm_ffn[layer])
        i = layer // 2
        if layer % 2 == 0:
            h = h + swiglu(hn, w_gate_dense[i], w_up_dense[i], w_down_dense[i])
        else:
            h = h + moe_swiglu(hn, w_router[i], w_gate_moe[i], w_up_moe[i], w_down_moe[i])
    return rmsnorm(h, norm_final)
```

```python
import functools
import math

import jax
import jax.numpy as jnp
from jax import lax
from jax.experimental import pallas as pl
from jax.experimental.pallas import tpu as pltpu

F32 = jnp.float32
BF16 = jnp.bfloat16

HEAD = 128
CHUNK = 64
SUB = 8
CONV_WIDTH = 4
DIL_PAIRS = ((128, 1), (512, 4), (2048, 16))
DIL_BLK = 128
DIL_SUPER = 2048
NUM_BUCKETS = 32
MAX_DISTANCE = 2048
TOP_K = 2
RMS_EPS = 1e-6
MASK_VALUE = -1e30
MIN_GATE = 1e-20
VMEM_LIMIT = 56 * 1024 * 1024


def _params(n_axes):
    return pltpu.CompilerParams(dimension_semantics=("arbitrary",) * n_axes,
                                vmem_limit_bytes=VMEM_LIMIT)


def _pick(n, cands):
    for c in cands:
        if n % c == 0:
            return c
    return n


def _mm(a, b):
    return jnp.dot(a.astype(BF16), b.astype(BF16), preferred_element_type=F32)


def _mm_nt(a, b):
    return lax.dot_general(a.astype(BF16), b.astype(BF16), (((1,), (1,)), ((), ())),
                           preferred_element_type=F32)


def _mm_tn(a, b):
    return lax.dot_general(a.astype(BF16), b.astype(BF16), (((0,), (0,)), ((), ())),
                           preferred_element_type=F32)


def _mm_hi(a, b):
    return jnp.dot(a, b, precision=lax.Precision.HIGHEST, preferred_element_type=F32)


def _sigmoid(x):
    return 1.0 / (1.0 + jnp.exp(-x))


def _silu(x):
    return x * _sigmoid(x)


def _softplus(x):
    return jnp.maximum(x, 0.0) + jnp.log(1.0 + jnp.exp(-jnp.abs(x)))


def _chunk_cumsum(x, rin):
    s = 1
    while s < CHUNK:
        x = x + jnp.where(rin >= s, pltpu.roll(x, s, 0), 0.0)
        s *= 2
    return x


def _rmsnorm_kernel(x_ref, g_ref, o_ref):
    x = x_ref[...]
    y = x * lax.rsqrt(jnp.mean(x * x, axis=-1, keepdims=True) + RMS_EPS)
    o_ref[...] = (y * g_ref[...]).astype(o_ref.dtype)


def _rmsnorm(x, gain, out_dtype):
    t, d = x.shape
    tm = _pick(t, (256, 128, 64, 8))
    return pl.pallas_call(
        _rmsnorm_kernel,
        grid=(t // tm,),
        in_specs=[pl.BlockSpec((tm, d), lambda i: (i, 0)),
                  pl.BlockSpec((1, d), lambda i: (0, 0))],
        out_specs=pl.BlockSpec((tm, d), lambda i: (i, 0)),
        out_shape=jax.ShapeDtypeStruct((t, d), out_dtype),
        compiler_params=_params(1),
        name="rmsnorm",
    )(x, gain.reshape(1, d).astype(F32))


def _matmul_kernel(*refs, nk, has_res, gate_lane):
    a_ref, b_ref = refs[0], refs[1]
    pos = 2
    r_ref = g_ref = None
    if has_res:
        r_ref = refs[pos]
        pos += 1
    if gate_lane is not None:
        g_ref = refs[pos]
        pos += 1
    o_ref = refs[pos]
    acc_ref = refs[pos + 1] if nk > 1 else None

    def finish(acc):
        if g_ref is not None:
            acc = acc * g_ref[:, gate_lane:gate_lane + 1]
        if r_ref is not None:
            acc = r_ref[...] + acc
        o_ref[...] = acc.astype(o_ref.dtype)

    part = jnp.dot(a_ref[...], b_ref[...], preferred_element_type=F32)
    if nk == 1:
        finish(part)
        return
    k = pl.program_id(2)

    @pl.when(k == 0)
    def _():
        acc_ref[...] = part

    @pl.when(k > 0)
    def _():
        acc_ref[...] += part

    @pl.when(k == nk - 1)
    def _():
        finish(acc_ref[...])


def _matmul(a, b, *, out_dtype, tm, tn, tk, residual=None, gates=None, gate_lane=None,
            expert=None, name="matmul"):
    m, kd = a.shape
    n = b.shape[-1]
    nk = kd // tk
    if expert is None:
        b_spec = pl.BlockSpec((tk, tn), lambda i, j, k: (k, j))
    else:
        b_spec = pl.BlockSpec((None, tk, tn), lambda i, j, k: (expert, k, j))
    in_specs = [pl.BlockSpec((tm, tk), lambda i, j, k: (i, k)), b_spec]
    args = [a, b]
    if residual is not None:
        in_specs.append(pl.BlockSpec((tm, tn), lambda i, j, k: (i, j)))
        args.append(residual)
    if gates is not None:
        in_specs.append(pl.BlockSpec((tm, gates.shape[1]), lambda i, j, k: (i, 0)))
        args.append(gates)
    return pl.pallas_call(
        functools.partial(_matmul_kernel, nk=nk, has_res=residual is not None,
                          gate_lane=gate_lane if gates is not None else None),
        grid=(m // tm, n // tn, nk),
        in_specs=in_specs,
        out_specs=pl.BlockSpec((tm, tn), lambda i, j, k: (i, j)),
        out_shape=jax.ShapeDtypeStruct((m, n), out_dtype),
        scratch_shapes=[pltpu.VMEM((tm, tn), F32)] if nk > 1 else [],
        compiler_params=_params(3),
        name=name,
    )(*args)


def _swiglu_up_kernel(a_ref, wg_ref, wu_ref, o_ref):
    a = a_ref[...]
    g = jnp.dot(a, wg_ref[...], preferred_element_type=F32)
    u = jnp.dot(a, wu_ref[...], preferred_element_type=F32)
    o_ref[...] = (_silu(g) * u).astype(o_ref.dtype)


def _swiglu_up(a, wg, wu, *, tm, tn, expert=None):
    m, kd = a.shape
    n = wg.shape[-1]
    if expert is None:
        w_spec = pl.BlockSpec((kd, tn), lambda i, j: (0, j))
    else:
        w_spec = pl.BlockSpec((None, kd, tn), lambda i, j: (expert, 0, j))
    return pl.pallas_call(
        _swiglu_up_kernel,
        grid=(m // tm, n // tn),
        in_specs=[pl.BlockSpec((tm, kd), lambda i, j: (i, 0)), w_spec, w_spec],
        out_specs=pl.BlockSpec((tm, tn), lambda i, j: (i, j)),
        out_shape=jax.ShapeDtypeStruct((m, n), BF16),
        compiler_params=_params(2),
        name="swiglu_up",
    )(a, wg, wu)


def _gdn_kernel(alog_ref, dtb_ref, q_ref, qh_ref, k_ref, kh_ref, v_ref, vh_ref, z_ref, t_ref,
                wq_ref, wk_ref, wv_ref, gain_ref, o_ref,
                s_ref, qs_ref, ks_ref, vs_ref, bs_ref, gs_ref, *, tb, n_heads):
    h = pl.program_id(1)
    i = pl.program_id(2)
    first = i == 0

    @pl.when(first)
    def _():
        s_ref[...] = jnp.zeros_like(s_ref)

    def conv_silu(x_ref, xh_ref, w_ref):
        halo = jnp.where(first, 0.0, xh_ref[...])
        ext = jnp.concatenate([halo, x_ref[...]], axis=0)
        w = w_ref[...]
        y = pltpu.roll(ext, 3, 0)[SUB:] * w[0:1]
        y = y + pltpu.roll(ext, 2, 0)[SUB:] * w[1:2]
        y = y + pltpu.roll(ext, 1, 0)[SUB:] * w[2:3]
        y = y + ext[SUB:] * w[3:4]
        return _silu(y)

    def l2norm(x):
        return x * lax.rsqrt(jnp.sum(x * x, axis=-1, keepdims=True) + RMS_EPS)

    qs_ref[...] = l2norm(conv_silu(q_ref, qh_ref, wq_ref)) * (HEAD ** -0.5)
    ks_ref[...] = l2norm(conv_silu(k_ref, kh_ref, wk_ref))
    vs_ref[...] = conv_silu(v_ref, vh_ref, wv_ref)

    tail = t_ref[...]
    lane = lax.broadcasted_iota(jnp.int32, (tb, HEAD), 1)
    b_col = jnp.sum(jnp.where(lane == h, tail, 0.0), axis=1, keepdims=True)
    a_col = jnp.sum(jnp.where(lane == h + n_heads, tail, 0.0), axis=1, keepdims=True)
    decay_rate = jnp.exp(jnp.zeros((1, 1), F32) + alog_ref[h])
    g = -decay_rate * _softplus(a_col + dtb_ref[h])
    rin = lax.broadcasted_iota(jnp.int32, (tb, HEAD), 0) & (CHUNK - 1)
    bs_ref[...] = jnp.broadcast_to(_sigmoid(b_col), (tb, HEAD))
    gs_ref[...] = _chunk_cumsum(jnp.broadcast_to(g, (tb, HEAD)), rin)

    row = lax.broadcasted_iota(jnp.int32, (CHUNK, CHUNK), 0)
    col = lax.broadcasted_iota(jnp.int32, (CHUNK, CHUNK), 1)
    eye = (row == col).astype(F32)
    causal = row >= col
    strict = row > col
    level_masks = [(((row >> k) == (col >> k)) & ((row >> (k - 1)) != (col >> (k - 1)))).astype(F32)
                   for k in range(1, 7)]
    gain = gain_ref[...]

    def chunk_body(c, carry):
        r0 = pl.multiple_of(c * CHUNK, CHUNK)
        q = qs_ref[pl.ds(r0, CHUNK), :]
        k = ks_ref[pl.ds(r0, CHUNK), :]
        v = vs_ref[pl.ds(r0, CHUNK), :]
        beta = bs_ref[pl.ds(r0, CHUNK), :]
        gc = gs_ref[pl.ds(r0, CHUNK), :]
        z = z_ref[pl.ds(r0, CHUNK), :]

        gcol = gc[:, :CHUNK]
        grow = jnp.sum(gcol * eye, axis=0, keepdims=True)
        decay = jnp.where(causal, jnp.exp(jnp.where(causal, gcol - grow, 0.0)), 0.0)
        egc = jnp.exp(gc)
        g_last = gc[CHUNK - 1:CHUNK, :]
        kb = k * beta
        lower = jnp.where(strict, _mm_nt(kb, k) * decay, 0.0)
        attn = _mm_nt(q, k) * decay

        inv = eye - lower * level_masks[0]
        for lvl in range(1, 6):
            inv = inv - _mm_hi(_mm_hi(inv, lower * level_masks[lvl]), inv)
        u = _mm_hi(inv, v * beta)
        w = _mm_hi(inv, kb * egc)

        state = s_ref[...]
        v_new = u - _mm(w, state)
        o = _mm(q * egc, state) + _mm(attn, v_new)
        k_dec = k * jnp.exp(g_last - gc)
        s_ref[...] = state * jnp.exp(g_last) + _mm_tn(k_dec, v_new)

        on = o * lax.rsqrt(jnp.mean(o * o, axis=-1, keepdims=True) + RMS_EPS) * gain
        o_ref[pl.ds(r0, CHUNK), :] = (on * _silu(z)).astype(o_ref.dtype)
        return carry

    lax.fori_loop(0, tb // CHUNK, chunk_body, 0)


def _gdn(proj, conv_w, a_log, dt_bias, gain, *, bsz, seq, n_heads, col_q, col_z, col_tail):
    tb = _pick(seq, (512, 256, 128, 64))
    nb = seq // tb

    def main(col):
        return pl.BlockSpec((tb, HEAD), lambda b, h, i: (b * nb + i, col + h))

    def halo(col):
        return pl.BlockSpec(
            (SUB, HEAD),
            lambda b, h, i: (jnp.maximum(b * (seq // SUB) + i * (tb // SUB) - 1, 0), col + h))

    def wspec(col):
        return pl.BlockSpec((CONV_WIDTH, HEAD), lambda b, h, i: (0, col + h))

    smem = pl.BlockSpec(memory_space=pltpu.SMEM)
    col_k = col_q + n_heads
    col_v = col_q + 2 * n_heads
    scratch = [pltpu.VMEM((HEAD, HEAD), F32)] + [pltpu.VMEM((tb, HEAD), F32)] * 5
    return pl.pallas_call(
        functools.partial(_gdn_kernel, tb=tb, n_heads=n_heads),
        grid=(bsz, n_heads, nb),
        in_specs=[smem, smem,
                  main(col_q), halo(col_q), main(col_k), halo(col_k), main(col_v), halo(col_v),
                  main(col_z),
                  pl.BlockSpec((tb, HEAD), lambda b, h, i: (b * nb + i, col_tail)),
                  wspec(0), wspec(n_heads), wspec(2 * n_heads),
                  pl.BlockSpec((1, HEAD), lambda b, h, i: (0, 0))],
        out_specs=pl.BlockSpec((tb, HEAD), lambda b, h, i: (b * nb + i, h)),
        out_shape=jax.ShapeDtypeStruct((bsz * seq, n_heads * HEAD), BF16),
        scratch_shapes=scratch,
        compiler_params=_params(3),
        name="gdn",
    )(a_log.astype(F32), dt_bias.astype(F32), proj, proj, proj, proj, proj, proj, proj, proj,
      conv_w, conv_w, conv_w, gain.reshape(1, HEAD).astype(F32))


def _hgrn_kernel(q_ref, f_ref, i_ref, g_ref, lb_ref, gain_ref, o_ref,
                 s_ref, qs_ref, ks_ref, bs_ref, *, tb, layer):
    @pl.when(pl.program_id(2) == 0)
    def _():
        s_ref[...] = jnp.zeros_like(s_ref)

    lbp = lb_ref[...]
    e = jnp.exp(lbp - jnp.max(lbp, axis=0, keepdims=True))
    probs = e / jnp.sum(e, axis=0, keepdims=True)
    lower = jnp.sum(probs[:layer + 1], axis=0, keepdims=True) - probs[0:1]

    f_pre = f_ref[...]
    f_gate = lower + (1.0 - lower) * _sigmoid(f_pre)
    log_f = jnp.log(jnp.maximum(f_gate, MIN_GATE))
    rin = lax.broadcasted_iota(jnp.int32, (tb, HEAD), 0) & (CHUNK - 1)
    bs_ref[...] = _chunk_cumsum(log_f, rin)
    ks_ref[...] = (1.0 - lower) * _sigmoid(-f_pre)
    qs_ref[...] = _silu(q_ref[...])

    row = lax.broadcasted_iota(jnp.int32, (CHUNK, CHUNK), 0)
    col = lax.broadcasted_iota(jnp.int32, (CHUNK, CHUNK), 1)
    sub_row = lax.broadcasted_iota(jnp.int32, (SUB, 1), 0)
    off_masks = {1 << sh: ((((row >> sh) & 1) == 1) & ((col >> sh) == (row >> sh) - 1)).astype(F32)
                 for sh in (3, 4, 5)}
    gain = gain_ref[...]
    n_sub = CHUNK // SUB

    def chunk_body(c, carry):
        r0 = pl.multiple_of(c * CHUNK, CHUNK)
        q = qs_ref[pl.ds(r0, CHUNK), :]
        k = ks_ref[pl.ds(r0, CHUNK), :]
        b = bs_ref[pl.ds(r0, CHUNK), :]
        v = i_ref[pl.ds(r0, CHUNK), :]
        gate = g_ref[pl.ds(r0, CHUNK), :]
        b_last = b[CHUNK - 1:CHUNK, :]
        state_t = s_ref[...]

        o = _mm_nt(q * jnp.exp(b), state_t)

        scores = jnp.zeros((CHUNK, CHUNK), F32)
        for m in (8, 16, 32):
            ref_q, ref_k = [], []
            for blk in range(n_sub):
                start = (blk * SUB // m) * m
                ref_q.append(jnp.broadcast_to(b[start:start + 1, :], (SUB, HEAD)))
                nxt = start + m
                if nxt < CHUNK:
                    ref_k.append(jnp.broadcast_to(b[nxt:nxt + 1, :], (SUB, HEAD)))
                else:
                    ref_k.append(b[blk * SUB:(blk + 1) * SUB, :])
            ref_q = jnp.concatenate(ref_q, axis=0)
            ref_k = jnp.concatenate(ref_k, axis=0)
            qe = q * jnp.exp(b - ref_q)
            ke = k * jnp.exp(ref_k - b)
            scores = scores + _mm_nt(qe, ke) * off_masks[m]
        o = o + _mm(scores, v)

        diag = []
        for blk in range(n_sub):
            sl = slice(blk * SUB, (blk + 1) * SUB)
            qi, ki, bi, vi = q[sl], k[sl], b[sl], v[sl]
            acc = jnp.zeros((SUB, HEAD), F32)
            for s in range(SUB):
                m = sub_row >= s
                dec = jnp.where(m, jnp.exp(jnp.where(m, bi - bi[s:s + 1], 0.0)), 0.0)
                wgt = jnp.sum(qi * ki[s:s + 1] * dec, axis=1, keepdims=True)
                acc = acc + wgt * vi[s:s + 1]
            diag.append(acc)
        o = o + jnp.concatenate(diag, axis=0)

        k_dec = k * jnp.exp(b_last - b)
        s_ref[...] = state_t * jnp.exp(b_last) + _mm_tn(v, k_dec)

        on = o * lax.rsqrt(jnp.mean(o * o, axis=-1, keepdims=True) + RMS_EPS) * gain
        o_ref[pl.ds(r0, CHUNK), :] = (on * _silu(gate)).astype(o_ref.dtype)
        return carry

    lax.fori_loop(0, tb // CHUNK, chunk_body, 0)


def _hgrn(proj, hgrn_lb, gain, *, bsz, seq, n_heads, col_q, layer):
    tb = _pick(seq, (512, 256, 128, 64))
    nb = seq // tb
    depth = hgrn_lb.shape[0]

    def main(col):
        return pl.BlockSpec((tb, HEAD), lambda b, h, i: (b * nb + i, col + h))

    scratch = [pltpu.VMEM((HEAD, HEAD), F32)] + [pltpu.VMEM((tb, HEAD), F32)] * 3
    return pl.pallas_call(
        functools.partial(_hgrn_kernel, tb=tb, layer=layer),
        grid=(bsz, n_heads, nb),
        in_specs=[main(col_q), main(col_q + n_heads), main(col_q + 2 * n_heads),
                  main(col_q + 3 * n_heads),
                  pl.BlockSpec((depth, HEAD), lambda b, h, i: (0, h)),
                  pl.BlockSpec((1, HEAD), lambda b, h, i: (0, 0))],
        out_specs=pl.BlockSpec((tb, HEAD), lambda b, h, i: (b * nb + i, h)),
        out_shape=jax.ShapeDtypeStruct((bsz * seq, n_heads * HEAD), BF16),
        scratch_shapes=scratch,
        compiler_params=_params(3),
        name="hgrn2",
    )(proj, proj, proj, proj, hgrn_lb.astype(F32), gain.reshape(1, HEAD).astype(F32))


def _dil_kernel(q0_ref, q1_ref, q2_ref, kp_ref, kc_ref, vp_ref, vc_ref,
                b0_ref, b1_ref, b2_ref, o_ref, o_scr, l_scr):
    first_super = pl.program_id(2) == 0
    q_refs = (q0_ref, q1_ref, q2_ref)
    b_refs = (b0_ref, b1_ref, b2_ref)
    qi = lax.broadcasted_iota(jnp.int32, (DIL_BLK, DIL_BLK), 0)
    ki = lax.broadcasted_iota(jnp.int32, (DIL_BLK, DIL_BLK), 1)
    valid_prev = ki >= qi
    valid_cur = ki <= qi
    scale = HEAD ** -0.5

    def rows(ref, start, dil):
        if dil == 1:
            return ref[pl.ds(start, DIL_BLK), :]
        return ref[pl.ds(start, DIL_BLK, stride=dil), :]

    for g, (window, dil) in enumerate(DIL_PAIRS):
        span = DIL_BLK * dil
        bias_p = b_refs[g][:, :DIL_BLK]
        bias_c = b_refs[g][:, DIL_BLK:]
        for s in range(DIL_SUPER // span):
            for c in range(dil):
                start = s * span + c
                q = rows(q_refs[g], start, dil)
                k_cur = rows(kc_ref, start, dil)
                v_cur = rows(vc_ref, start, dil)
                if s == 0:
                    k_prev = rows(kp_ref, DIL_SUPER - span + c, dil)
                    v_prev = rows(vp_ref, DIL_SUPER - span + c, dil)
                else:
                    k_prev = rows(kc_ref, start - span, dil)
                    v_prev = rows(vc_ref, start - span, dil)
                lp = jnp.where(valid_prev, _mm_nt(q, k_prev) * scale + bias_p, MASK_VALUE)
                if s == 0:
                    lp = jnp.where(first_super, MASK_VALUE, lp)
                lc = jnp.where(valid_cur, _mm_nt(q, k_cur) * scale + bias_c, MASK_VALUE)
                mx = jnp.maximum(jnp.max(lp, axis=1, keepdims=True),
                                 jnp.max(lc, axis=1, keepdims=True))
                pp = jnp.exp(lp - mx)
                pc = jnp.exp(lc - mx)
                den = jnp.sum(pp, axis=1, keepdims=True) + jnp.sum(pc, axis=1, keepdims=True)
                out = (_mm(pp, v_prev) + _mm(pc, v_cur)) / den
                log_den = jnp.broadcast_to(mx + jnp.log(den), (DIL_BLK, HEAD))
                if dil == 1:
                    o_scr[g, pl.ds(start, DIL_BLK), :] = out
                    l_scr[g, pl.ds(start, DIL_BLK), :] = log_den
                else:
                    o_scr[g, pl.ds(start, DIL_BLK, stride=dil), :] = out
                    l_scr[g, pl.ds(start, DIL_BLK, stride=dil), :] = log_den

    l0, l1, l2 = l_scr[0], l_scr[1], l_scr[2]
    mx = jnp.maximum(jnp.maximum(l0, l1), l2)
    w0, w1, w2 = jnp.exp(l0 - mx), jnp.exp(l1 - mx), jnp.exp(l2 - mx)
    mixed = (w0 * o_scr[0] + w1 * o_scr[1] + w2 * o_scr[2]) / (w0 + w1 + w2)
    o_ref[...] = mixed.astype(o_ref.dtype)


def _t5_bucket(dist):
    max_exact = NUM_BUCKETS // 2
    d = jnp.maximum(dist, 1).astype(F32)
    large = max_exact + (jnp.log(d / max_exact) / math.log(MAX_DISTANCE / max_exact)
                         * (NUM_BUCKETS - max_exact)).astype(jnp.int32)
    large = jnp.clip(large, 0, NUM_BUCKETS - 1)
    return jnp.where(dist < max_exact, dist, large)


def _dil_bias_tables(rel_bias):
    n_kv = rel_bias.shape[1] // len(DIL_PAIRS)
    qi = jnp.arange(DIL_BLK)[:, None]
    ki = jnp.arange(2 * DIL_BLK)[None, :]
    steps = jnp.clip(qi + DIL_BLK - ki, 0, DIL_BLK)
    tables = []
    for g, (_, dil) in enumerate(DIL_PAIRS):
        tbl = rel_bias.astype(F32)[_t5_bucket(steps * dil)]
        tables.append(tbl[:, :, g * n_kv:(g + 1) * n_kv].transpose(2, 0, 1))
    return jnp.concatenate(tables, axis=0)


def _dilated(proj, bias_tbl, *, bsz, seq, n_kv, col_q, col_k, col_v):
    ns = seq // DIL_SUPER

    def qspec(g):
        return pl.BlockSpec((DIL_SUPER, HEAD), lambda b, h, j: (b * ns + j, col_q + g * n_kv + h))

    def cur(col):
        return pl.BlockSpec((DIL_SUPER, HEAD), lambda b, h, j: (b * ns + j, col + h))

    def prev(col):
        return pl.BlockSpec((DIL_SUPER, HEAD),
                            lambda b, h, j: (b * ns + jnp.maximum(j - 1, 0), col + h))

    def bspec(g):
        return pl.BlockSpec((None, DIL_BLK, 2 * DIL_BLK), lambda b, h, j: (g * n_kv + h, 0, 0))

    return pl.pallas_call(
        _dil_kernel,
        grid=(bsz, n_kv, ns),
        in_specs=[qspec(0), qspec(1), qspec(2), prev(col_k), cur(col_k), prev(col_v), cur(col_v),
                  bspec(0), bspec(1), bspec(2)],
        out_specs=pl.BlockSpec((DIL_SUPER, HEAD), lambda b, h, j: (b * ns + j, h)),
        out_shape=jax.ShapeDtypeStruct((bsz * seq, n_kv * HEAD), BF16),
        scratch_shapes=[pltpu.VMEM((len(DIL_PAIRS), DIL_SUPER, HEAD), F32)] * 2,
        compiler_params=_params(3),
        name="dilated_attn",
    )(proj, proj, proj, proj, proj, proj, proj, bias_tbl, bias_tbl, bias_tbl)


def _router_kernel(x_ref, g_ref, w_ref, hn_ref, gates_ref, *, n_experts):
    x = x_ref[...]
    hn = x * lax.rsqrt(jnp.mean(x * x, axis=-1, keepdims=True) + RMS_EPS) * g_ref[...]
    hn_ref[...] = hn.astype(hn_ref.dtype)
    logits = _mm_hi(hn, w_ref[...])
    lane = lax.broadcasted_iota(jnp.int32, logits.shape, 1)
    logits = jnp.where(lane < n_experts, logits, -jnp.inf)
    m1 = jnp.max(logits, axis=1, keepdims=True)
    i1 = jnp.min(jnp.where(logits == m1, lane, HEAD), axis=1, keepdims=True)
    rest = jnp.where(lane == i1, -jnp.inf, logits)
    m2 = jnp.max(rest, axis=1, keepdims=True)
    i2 = jnp.min(jnp.where(rest == m2, lane, HEAD), axis=1, keepdims=True)
    e2 = jnp.exp(m2 - m1)
    w1 = 1.0 / (1.0 + e2)
    w2 = e2 / (1.0 + e2)
    gates_ref[...] = jnp.where(lane == i1, w1, 0.0) + jnp.where(lane == i2, w2, 0.0)


def _router(x, gain, w_router):
    t, d = x.shape
    n_experts = w_router.shape[1]
    tm = _pick(t, (256, 128, 64, 8))
    w_pad = jnp.zeros((d, HEAD), F32).at[:, :n_experts].set(w_router.astype(F32))
    return pl.pallas_call(
        functools.partial(_router_kernel, n_experts=n_experts),
        grid=(t // tm,),
        in_specs=[pl.BlockSpec((tm, d), lambda i: (i, 0)),
                  pl.BlockSpec((1, d), lambda i: (0, 0)),
                  pl.BlockSpec((d, HEAD), lambda i: (0, 0))],
        out_specs=[pl.BlockSpec((tm, d), lambda i: (i, 0)),
                   pl.BlockSpec((tm, HEAD), lambda i: (i, 0))],
        out_shape=[jax.ShapeDtypeStruct((t, d), BF16), jax.ShapeDtypeStruct((t, HEAD), F32)],
        compiler_params=_params(1),
        name="router",
    )(x, gain.reshape(1, d).astype(F32), w_pad)


def _pad_in_proj(w, n_gdn):
    d = w.shape[0]
    split = 4 * n_gdn * HEAD
    main = jnp.concatenate([w[:, :split], w[:, split + 2 * n_gdn:]], axis=1)
    n_main = main.shape[1] // HEAD
    total = n_main + 1
    total = -(-total // 8) * 8
    pad = jnp.zeros((d, total * HEAD - main.shape[1] - 2 * n_gdn), w.dtype)
    return jnp.concatenate([main, w[:, split:split + 2 * n_gdn], pad], axis=1).astype(BF16), n_main


def kernel(x, w_in, conv_gdn, gdn_a_log, gdn_dt_bias, gdn_norm, hgrn_lb, hgrn_norm, rel_bias,
           w_out, norm_mix, norm_ffn, w_gate_dense, w_up_dense, w_down_dense, w_router,
           w_gate_moe, w_up_moe, w_down_moe, norm_final):
    bsz, seq, d = x.shape
    depth = w_in.shape[0]
    t = bsz * seq
    n_gdn = gdn_a_log.shape[1]
    n_hgrn = hgrn_lb.shape[1] // HEAD
    n_q = rel_bias.shape[1]
    n_kv = n_q // len(DIL_PAIRS)
    n_experts = w_router.shape[-1]

    col_gq = 0
    col_gz = 3 * n_gdn
    col_hq = 4 * n_gdn
    col_dq = col_hq + 4 * n_hgrn
    col_dk = col_dq + n_q
    col_dv = col_dk + n_kv

    bias_tbl = _dil_bias_tables(rel_bias)
    tm = _pick(t, (1024, 512, 256, 128))

    h = x.reshape(t, d).astype(F32)
    for layer in range(depth):
        hn = _rmsnorm(h, norm_mix[layer], BF16)
        w_pad, col_tail = _pad_in_proj(w_in[layer], n_gdn)
        n_in = w_pad.shape[1]
        proj = _matmul(hn, w_pad, out_dtype=F32, tm=tm, tn=_pick(n_in, (1024, 512, 256, 128)),
                       tk=d, name="in_proj")
        oa = _gdn(proj, conv_gdn[layer].astype(F32), gdn_a_log[layer], gdn_dt_bias[layer],
                  gdn_norm[layer], bsz=bsz, seq=seq, n_heads=n_gdn, col_q=col_gq, col_z=col_gz,
                  col_tail=col_tail)
        ob = _hgrn(proj, hgrn_lb, hgrn_norm[layer], bsz=bsz, seq=seq, n_heads=n_hgrn,
                   col_q=col_hq, layer=layer)
        oc = _dilated(proj, bias_tbl, bsz=bsz, seq=seq, n_kv=n_kv, col_q=col_dq, col_k=col_dk,
                      col_v=col_dv)
        mixed = jnp.concatenate([oa, ob, oc], axis=1)
        k_out = mixed.shape[1]
        h = _matmul(mixed, w_out[layer].astype(BF16), out_dtype=F32, tm=tm,
                    tn=_pick(d, (1024, 512, 256, 128)), tk=_pick(k_out, (3072, 1536, 1024, 512)),
                    residual=h, name="out_proj")

        idx = layer // 2
        if layer % 2 == 0:
            hn = _rmsnorm(h, norm_ffn[layer], BF16)
            d_ff = w_gate_dense.shape[-1]
            act = _swiglu_up(hn, w_gate_dense[idx].astype(BF16), w_up_dense[idx].astype(BF16),
                             tm=tm, tn=_pick(d_ff, (512, 256, 128)))
            h = _matmul(act, w_down_dense[idx].astype(BF16), out_dtype=F32, tm=tm,
                        tn=_pick(d, (1024, 512, 256, 128)),
                        tk=_pick(d_ff, (2048, 1792, 1024, 512, 256, 128)), residual=h,
                        name="ffn_down")
        else:
            hn, gates = _router(h, norm_ffn[layer], w_router[idx])
            d_ff = w_gate_moe.shape[-1]
            wg = w_gate_moe[idx].astype(BF16)
            wu = w_up_moe[idx].astype(BF16)
            wd = w_down_moe[idx].astype(BF16)
            for e in range(n_experts):
                act = _swiglu_up(hn, wg, wu, tm=tm, tn=_pick(d_ff, (512, 256, 128)), expert=e)
                h = _matmul(act, wd, out_dtype=F32, tm=tm, tn=_pick(d, (1024, 512, 256, 128)),
                            tk=_pick(d_ff, (2816, 1408, 512, 256, 128)), residual=h, gates=gates,
                            gate_lane=e, expert=e, name="moe_down")

    out = _rmsnorm(h, norm_final, x.dtype)
    return out.reshape(bsz, seq, d)
```

```python
import functools
import math

import jax
import jax.numpy as jnp
from jax import lax
from jax.experimental import pallas as pl
from jax.experimental.pallas import tpu as pltpu

F32 = jnp.float32
BF16 = jnp.bfloat16

HEAD = 128
CHUNK = 64
SUB = 8
CONV_WIDTH = 4
DIL_PAIRS = ((128, 1), (512, 4), (2048, 16))
DIL_BLK = 128
DIL_SUPER = 2048
NUM_BUCKETS = 32
MAX_DISTANCE = 2048
TOP_K = 2
RMS_EPS = 1e-6
MASK_VALUE = -1e30
MIN_GATE = 1e-20
VMEM_LIMIT = 56 * 1024 * 1024


def _params(n_axes):
    return pltpu.CompilerParams(dimension_semantics=("arbitrary",) * n_axes,
                                vmem_limit_bytes=VMEM_LIMIT)


def _pick(n, cands):
    for c in cands:
        if n % c == 0:
            return c
    return n


def _dot(a, b):
    return jnp.dot(a, b, preferred_element_type=F32)


def _mm(a, b):
    return _dot(a.astype(BF16), b.astype(BF16))


def _mm_nt(a, b):
    return lax.dot_general(a.astype(BF16), b.astype(BF16), (((1,), (1,)), ((), ())),
                           preferred_element_type=F32)


def _mm_tn(a, b):
    return lax.dot_general(a.astype(BF16), b.astype(BF16), (((0,), (0,)), ((), ())),
                           preferred_element_type=F32)


def _mm_hi(a, b):
    return jnp.dot(a, b, precision=lax.Precision.HIGHEST, preferred_element_type=F32)


def _split(x):
    hi = x.astype(BF16)
    return hi, (x - hi.astype(F32)).astype(BF16)


def _mm_split(a, b):
    return _dot(a[0], b[0]) + (_dot(a[0], b[1]) + _dot(a[1], b[0]))


def _sigmoid(x):
    return 1.0 / (1.0 + jnp.exp(-x))


def _silu(x):
    return x * _sigmoid(x)


def _softplus(x):
    return jnp.maximum(x, 0.0) + jnp.log(1.0 + jnp.exp(-jnp.abs(x)))


def _rms(x, gain):
    return x * lax.rsqrt(jnp.mean(x * x, axis=-1, keepdims=True) + RMS_EPS) * gain


def _chunk_cumsum(x, rin):
    s = 1
    while s < CHUNK:
        x = x + jnp.where(rin >= s, pltpu.roll(x, s, 0), 0.0)
        s *= 2
    return x


def _rmsnorm_kernel(x_ref, g_ref, o_ref):
    o_ref[...] = _rms(x_ref[...], g_ref[...]).astype(o_ref.dtype)


def _rmsnorm(x, gain, out_dtype):
    t, d = x.shape
    tm = _pick(t, (256, 128, 64, 8))
    return pl.pallas_call(
        _rmsnorm_kernel,
        grid=(t // tm,),
        in_specs=[pl.BlockSpec((tm, d), lambda i: (i, 0)),
                  pl.BlockSpec((1, d), lambda i: (0, 0))],
        out_specs=pl.BlockSpec((tm, d), lambda i: (i, 0)),
        out_shape=jax.ShapeDtypeStruct((t, d), out_dtype),
        compiler_params=_params(1),
        name="rmsnorm",
    )(x, gain.reshape(1, d).astype(F32))


def _matmul_kernel(*refs, nk, has_res):
    a_ref, b_ref = refs[0], refs[1]
    r_ref = refs[2] if has_res else None
    o_ref = refs[3] if has_res else refs[2]
    acc_ref = refs[-1] if nk > 1 else None

    def finish(acc):
        if r_ref is not None:
            acc = r_ref[...] + acc
        o_ref[...] = acc.astype(o_ref.dtype)

    part = _dot(a_ref[...], b_ref[...].astype(BF16))
    if nk == 1:
        finish(part)
        return
    k = pl.program_id(2)

    @pl.when(k == 0)
    def _():
        acc_ref[...] = part

    @pl.when(k > 0)
    def _():
        acc_ref[...] += part

    @pl.when(k == nk - 1)
    def _():
        finish(acc_ref[...])


def _matmul(a, b, *, out_dtype, tm, tn, tk, residual=None, name="matmul"):
    m, kd = a.shape
    n = b.shape[-1]
    nk = kd // tk
    in_specs = [pl.BlockSpec((tm, tk), lambda i, j, k: (i, k)),
                pl.BlockSpec((tk, tn), lambda i, j, k: (k, j))]
    args = [a, b]
    if residual is not None:
        in_specs.append(pl.BlockSpec((tm, tn), lambda i, j, k: (i, j)))
        args.append(residual)
    return pl.pallas_call(
        functools.partial(_matmul_kernel, nk=nk, has_res=residual is not None),
        grid=(m // tm, n // tn, nk),
        in_specs=in_specs,
        out_specs=pl.BlockSpec((tm, tn), lambda i, j, k: (i, j)),
        out_shape=jax.ShapeDtypeStruct((m, n), out_dtype),
        scratch_shapes=[pltpu.VMEM((tm, tn), F32)] if nk > 1 else [],
        compiler_params=_params(3),
        name=name,
    )(*args)


def _swiglu_up_kernel(a_ref, wg_ref, wu_ref, o_ref):
    a = a_ref[...]
    g = _dot(a, wg_ref[...].astype(BF16))
    u = _dot(a, wu_ref[...].astype(BF16))
    o_ref[...] = (_silu(g) * u).astype(o_ref.dtype)


def _swiglu_up(a, wg, wu, *, tm, tn):
    m, kd = a.shape
    n = wg.shape[-1]
    w_spec = pl.BlockSpec((kd, tn), lambda i, j: (0, j))
    return pl.pallas_call(
        _swiglu_up_kernel,
        grid=(m // tm, n // tn),
        in_specs=[pl.BlockSpec((tm, kd), lambda i, j: (i, 0)), w_spec, w_spec],
        out_specs=pl.BlockSpec((tm, tn), lambda i, j: (i, j)),
        out_shape=jax.ShapeDtypeStruct((m, n), BF16),
        compiler_params=_params(2),
        name="swiglu_up",
    )(a, wg, wu)


def _out_proj_kernel(*refs, n_pieces):
    a_refs = refs[:n_pieces]
    w_refs = refs[n_pieces:2 * n_pieces]
    r_ref, o_ref = refs[2 * n_pieces], refs[2 * n_pieces + 1]
    acc = r_ref[...]
    for a_ref, w_ref in zip(a_refs, w_refs):
        acc = acc + _dot(a_ref[...], w_ref[...].astype(BF16))
    o_ref[...] = acc


def _out_proj(parts, w, residual, *, tm, tn):
    m = residual.shape[0]
    n = w.shape[1]
    unit = HEAD * functools.reduce(math.gcd, [p.shape[1] // HEAD for p in parts])
    a_specs, a_args = [], []
    for p in parts:
        for c in range(p.shape[1] // unit):
            a_specs.append(pl.BlockSpec((tm, unit), lambda i, j, c=c: (i, c)))
            a_args.append(p)
    n_pieces = len(a_args)
    w_specs = [pl.BlockSpec((unit, tn), lambda i, j, c=c: (c, j)) for c in range(n_pieces)]
    return pl.pallas_call(
        functools.partial(_out_proj_kernel, n_pieces=n_pieces),
        grid=(m // tm, n // tn),
        in_specs=a_specs + w_specs + [pl.BlockSpec((tm, tn), lambda i, j: (i, j))],
        out_specs=pl.BlockSpec((tm, tn), lambda i, j: (i, j)),
        out_shape=jax.ShapeDtypeStruct((m, n), F32),
        compiler_params=_params(2),
        name="out_proj",
    )(*a_args, *([w] * n_pieces), residual)


def _gdn_kernel(alog_ref, dtb_ref, q_ref, qh_ref, k_ref, kh_ref, v_ref, vh_ref, z_ref, t_ref,
                wq_ref, wk_ref, wv_ref, gain_ref, o_ref, s_ref, *, tb, n_heads, group):
    hg = pl.program_id(1)
    first = pl.program_id(2) == 0

    @pl.when(first)
    def _():
        s_ref[...] = jnp.zeros_like(s_ref)

    def conv_silu(x_ref, xh_ref, w_ref):
        halo = jnp.where(first, 0.0, xh_ref[...])
        ext = jnp.concatenate([halo, x_ref[...]], axis=0)
        w = w_ref[...]
        y = pltpu.roll(ext, 3, 0)[SUB:] * w[0:1]
        y = y + pltpu.roll(ext, 2, 0)[SUB:] * w[1:2]
        y = y + pltpu.roll(ext, 1, 0)[SUB:] * w[2:3]
        y = y + ext[SUB:] * w[3:4]
        return _silu(y)

    def l2norm(x):
        return x * lax.rsqrt(jnp.sum(x * x, axis=-1, keepdims=True) + RMS_EPS)

    q_all = conv_silu(q_ref, qh_ref, wq_ref)
    k_all = conv_silu(k_ref, kh_ref, wk_ref)
    v_all = conv_silu(v_ref, vh_ref, wv_ref)
    tail = t_ref[...]
    lane = lax.broadcasted_iota(jnp.int32, (tb, HEAD), 1)
    rin = lax.broadcasted_iota(jnp.int32, (tb, HEAD), 0) & (CHUNK - 1)

    row = lax.broadcasted_iota(jnp.int32, (CHUNK, CHUNK), 0)
    col = lax.broadcasted_iota(jnp.int32, (CHUNK, CHUNK), 1)
    eye = (row == col).astype(F32)
    causal = row >= col
    strict = row > col
    level_masks = [(((row >> k) == (col >> k)) & ((row >> (k - 1)) != (col >> (k - 1)))).astype(F32)
                   for k in range(1, 7)]

    n_chunks = tb // CHUNK
    units = []
    for g in range(group):
        h = hg * group + g
        hs = slice(g * HEAD, (g + 1) * HEAD)
        q_h = l2norm(q_all[:, hs]) * (HEAD ** -0.5)
        k_h = l2norm(k_all[:, hs])
        v_h = v_all[:, hs]
        b_col = jnp.sum(jnp.where(lane == h, tail, 0.0), axis=1, keepdims=True)
        a_col = jnp.sum(jnp.where(lane == h + n_heads, tail, 0.0), axis=1, keepdims=True)
        decay_rate = jnp.exp(jnp.zeros((1, 1), F32) + alog_ref[h])
        log_decay = -decay_rate * _softplus(a_col + dtb_ref[h])
        beta_h = jnp.broadcast_to(_sigmoid(b_col), (tb, HEAD))
        gc_h = _chunk_cumsum(jnp.broadcast_to(log_decay, (tb, HEAD)), rin)
        for c in range(n_chunks):
            sl = slice(c * CHUNK, (c + 1) * CHUNK)
            units.append(dict(q=q_h[sl], k=k_h[sl], v=v_h[sl], beta=beta_h[sl], gc=gc_h[sl]))

    for p in units:
        gcol = p["gc"][:, :CHUNK]
        grow = jnp.sum(gcol * eye, axis=0, keepdims=True)
        p["decay"] = jnp.where(causal, jnp.exp(jnp.where(causal, gcol - grow, 0.0)), 0.0)
        p["egc"] = jnp.exp(p["gc"])
        p["kb"] = p["k"] * p["beta"]
    for p in units:
        p["lower"] = jnp.where(strict, _mm_nt(p["kb"], p["k"]) * p["decay"], 0.0)
        p["attn"] = _mm_nt(p["q"], p["k"]) * p["decay"]
        p["inv"] = eye - p["lower"] * level_masks[0]
    for lvl in range(1, 6):
        for p in units:
            p["inv_s"] = _split(p["inv"])
            p["step"] = _mm_split(p["inv_s"], _split(p["lower"] * level_masks[lvl]))
        for p in units:
            p["inv"] = p["inv"] - _mm_split(_split(p["step"]), p["inv_s"])
    for p in units:
        inv_s = _split(p["inv"])
        p["u"] = _mm_split(inv_s, _split(p["v"] * p["beta"]))
        p["w"] = _mm_split(inv_s, _split(p["kb"] * p["egc"]))
        g_last = p["gc"][CHUNK - 1:CHUNK, :]
        p["q_dec"] = p["q"] * p["egc"]
        p["k_dec"] = p["k"] * jnp.exp(g_last - p["gc"])
        p["e_last"] = jnp.exp(g_last)

    gain = gain_ref[...]
    states = [s_ref[g] for g in range(group)]
    for c in range(n_chunks):
        ps = [units[g * n_chunks + c] for g in range(group)]
        v_new = [p["u"] - _mm(p["w"], s) for p, s in zip(ps, states)]
        outs = [_mm(p["q_dec"], s) + _mm(p["attn"], vn) for p, s, vn in zip(ps, states, v_new)]
        states = [s * p["e_last"] + _mm_tn(p["k_dec"], vn) for p, s, vn in zip(ps, states, v_new)]
        sl = slice(c * CHUNK, (c + 1) * CHUNK)
        for g, o in enumerate(outs):
            hs = slice(g * HEAD, (g + 1) * HEAD)
            o_ref[sl, hs] = (_rms(o, gain) * _silu(z_ref[sl, hs])).astype(o_ref.dtype)
    for g in range(group):
        s_ref[g] = states[g]


def _gdn(proj, conv_w, a_log, dt_bias, gain, *, bsz, seq, n_heads, col_q, col_z, col_tail):
    group = _pick(n_heads, (4, 3, 2))
    tb = _pick(seq, (128, 64))
    nb = seq // tb
    width = group * HEAD

    def main(col):
        return pl.BlockSpec((tb, width), lambda b, h, i: (b * nb + i, col // group + h))

    def halo(col):
        return pl.BlockSpec(
            (SUB, width),
            lambda b, h, i: (jnp.maximum(b * (seq // SUB) + i * (tb // SUB) - 1, 0),
                             col // group + h))

    def wspec(col):
        return pl.BlockSpec((CONV_WIDTH, width), lambda b, h, i: (0, col // group + h))

    smem = pl.BlockSpec(memory_space=pltpu.SMEM)
    col_k = col_q + n_heads
    col_v = col_q + 2 * n_heads
    assert all(c % group == 0 for c in (col_q, col_k, col_v, col_z))
    return pl.pallas_call(
        functools.partial(_gdn_kernel, tb=tb, n_heads=n_heads, group=group),
        grid=(bsz, n_heads // group, nb),
        in_specs=[smem, smem,
                  main(col_q), halo(col_q), main(col_k), halo(col_k), main(col_v), halo(col_v),
                  main(col_z),
                  pl.BlockSpec((tb, HEAD), lambda b, h, i: (b * nb + i, col_tail)),
                  wspec(0), wspec(n_heads), wspec(2 * n_heads),
                  pl.BlockSpec((1, HEAD), lambda b, h, i: (0, 0))],
        out_specs=pl.BlockSpec((tb, width), lambda b, h, i: (b * nb + i, h)),
        out_shape=jax.ShapeDtypeStruct((bsz * seq, n_heads * HEAD), BF16),
        scratch_shapes=[pltpu.VMEM((group, HEAD, HEAD), F32)],
        compiler_params=_params(3),
        name="gdn",
    )(a_log.astype(F32), dt_bias.astype(F32), proj, proj, proj, proj, proj, proj, proj, proj,
      conv_w, conv_w, conv_w, gain.reshape(1, HEAD).astype(F32))


def _hgrn_kernel(q_ref, f_ref, i_ref, g_ref, lb_ref, gain_ref, o_ref,
                 s_ref, qs_ref, ks_ref, bs_ref, *, tb, layer):
    @pl.when(pl.program_id(2) == 0)
    def _():
        s_ref[...] = jnp.zeros_like(s_ref)

    lbp = lb_ref[...]
    e = jnp.exp(lbp - jnp.max(lbp, axis=0, keepdims=True))
    probs = e / jnp.sum(e, axis=0, keepdims=True)
    lower = jnp.sum(probs[:layer + 1], axis=0, keepdims=True) - probs[0:1]

    f_pre = f_ref[...]
    f_gate = lower + (1.0 - lower) * _sigmoid(f_pre)
    log_f = jnp.log(jnp.maximum(f_gate, MIN_GATE))
    rin = lax.broadcasted_iota(jnp.int32, (tb, HEAD), 0) & (CHUNK - 1)
    bs_ref[...] = _chunk_cumsum(log_f, rin)
    ks_ref[...] = (1.0 - lower) * _sigmoid(-f_pre)
    qs_ref[...] = _silu(q_ref[...])

    row = lax.broadcasted_iota(jnp.int32, (CHUNK, CHUNK), 0)
    col = lax.broadcasted_iota(jnp.int32, (CHUNK, CHUNK), 1)
    sub_row = lax.broadcasted_iota(jnp.int32, (SUB, 1), 0)
    off_masks = {1 << sh: ((((row >> sh) & 1) == 1) & ((col >> sh) == (row >> sh) - 1)).astype(F32)
                 for sh in (3, 4, 5)}
    gain = gain_ref[...]
    n_sub = CHUNK // SUB

    def chunk_body(c, carry):
        r0 = pl.multiple_of(c * CHUNK, CHUNK)
        q = qs_ref[pl.ds(r0, CHUNK), :]
        k = ks_ref[pl.ds(r0, CHUNK), :]
        b = bs_ref[pl.ds(r0, CHUNK), :]
        v = i_ref[pl.ds(r0, CHUNK), :]
        gate = g_ref[pl.ds(r0, CHUNK), :]
        b_last = b[CHUNK - 1:CHUNK, :]
        state_t = s_ref[...]

        o = _mm_nt(q * jnp.exp(b), state_t)

        scores = jnp.zeros((CHUNK, CHUNK), F32)
        for m in (8, 16, 32):
            ref_q, ref_k = [], []
            for blk in range(n_sub):
                start = (blk * SUB // m) * m
                ref_q.append(jnp.broadcast_to(b[start:start + 1, :], (SUB, HEAD)))
                nxt = start + m
                if nxt < CHUNK:
                    ref_k.append(jnp.broadcast_to(b[nxt:nxt + 1, :], (SUB, HEAD)))
                else:
                    ref_k.append(b[blk * SUB:(blk + 1) * SUB, :])
            ref_q = jnp.concatenate(ref_q, axis=0)
            ref_k = jnp.concatenate(ref_k, axis=0)
            qe = q * jnp.exp(b - ref_q)
            ke = k * jnp.exp(ref_k - b)
            scores = scores + _mm_nt(qe, ke) * off_masks[m]
        o = o + _mm(scores, v)

        diag = []
        for blk in range(n_sub):
            sl = slice(blk * SUB, (blk + 1) * SUB)
            qi, ki, bi, vi = q[sl], k[sl], b[sl], v[sl]
            acc = jnp.zeros((SUB, HEAD), F32)
            for s in range(SUB):
                m = sub_row >= s
                dec = jnp.where(m, jnp.exp(jnp.where(m, bi - bi[s:s + 1], 0.0)), 0.0)
                wgt = jnp.sum(qi * ki[s:s + 1] * dec, axis=1, keepdims=True)
                acc = acc + wgt * vi[s:s + 1]
            diag.append(acc)
        o = o + jnp.concatenate(diag, axis=0)

        k_dec = k * jnp.exp(b_last - b)
        s_ref[...] = state_t * jnp.exp(b_last) + _mm_tn(v, k_dec)

        o_ref[pl.ds(r0, CHUNK), :] = (_rms(o, gain) * _silu(gate)).astype(o_ref.dtype)
        return carry

    lax.fori_loop(0, tb // CHUNK, chunk_body, 0)


def _hgrn(proj, hgrn_lb, gain, *, bsz, seq, n_heads, col_q, layer):
    tb = _pick(seq, (512, 256, 128, 64))
    nb = seq // tb
    depth = hgrn_lb.shape[0]

    def main(col):
        return pl.BlockSpec((tb, HEAD), lambda b, h, i: (b * nb + i, col + h))

    scratch = [pltpu.VMEM((HEAD, HEAD), F32)] + [pltpu.VMEM((tb, HEAD), F32)] * 3
    return pl.pallas_call(
        functools.partial(_hgrn_kernel, tb=tb, layer=layer),
        grid=(bsz, n_heads, nb),
        in_specs=[main(col_q), main(col_q + n_heads), main(col_q + 2 * n_heads),
                  main(col_q + 3 * n_heads),
                  pl.BlockSpec((depth, HEAD), lambda b, h, i: (0, h)),
                  pl.BlockSpec((1, HEAD), lambda b, h, i: (0, 0))],
        out_specs=pl.BlockSpec((tb, HEAD), lambda b, h, i: (b * nb + i, h)),
        out_shape=jax.ShapeDtypeStruct((bsz * seq, n_heads * HEAD), BF16),
        scratch_shapes=scratch,
        compiler_params=_params(3),
        name="hgrn2",
    )(proj, proj, proj, proj, hgrn_lb.astype(F32), gain.reshape(1, HEAD).astype(F32))


def _dil_kernel(q0_ref, q1_ref, q2_ref, kp_ref, kc_ref, vp_ref, vc_ref,
                b0_ref, b1_ref, b2_ref, o_ref, o_scr, l_scr):
    first_super = pl.program_id(2) == 0
    q_refs = (q0_ref, q1_ref, q2_ref)
    b_refs = (b0_ref, b1_ref, b2_ref)
    qi = lax.broadcasted_iota(jnp.int32, (DIL_BLK, DIL_BLK), 0)
    ki = lax.broadcasted_iota(jnp.int32, (DIL_BLK, DIL_BLK), 1)
    valid_prev = ki >= qi
    valid_cur = ki <= qi
    scale = HEAD ** -0.5

    def rows(ref, start, dil):
        if dil == 1:
            return ref[pl.ds(start, DIL_BLK), :]
        return ref[pl.ds(start, DIL_BLK, stride=dil), :]

    for g, (window, dil) in enumerate(DIL_PAIRS):
        span = DIL_BLK * dil
        bias_p = b_refs[g][:, :DIL_BLK]
        bias_c = b_refs[g][:, DIL_BLK:]
        for s in range(DIL_SUPER // span):
            for c in range(dil):
                start = s * span + c
                q = rows(q_refs[g], start, dil)
                k_cur = rows(kc_ref, start, dil)
                v_cur = rows(vc_ref, start, dil)
                if s == 0:
                    k_prev = rows(kp_ref, DIL_SUPER - span + c, dil)
                    v_prev = rows(vp_ref, DIL_SUPER - span + c, dil)
                else:
                    k_prev = rows(kc_ref, start - span, dil)
                    v_prev = rows(vc_ref, start - span, dil)
                lp = jnp.where(valid_prev, _mm_nt(q, k_prev) * scale + bias_p, MASK_VALUE)
                if s == 0:
                    lp = jnp.where(first_super, MASK_VALUE, lp)
                lc = jnp.where(valid_cur, _mm_nt(q, k_cur) * scale + bias_c, MASK_VALUE)
                mx = jnp.maximum(jnp.max(lp, axis=1, keepdims=True),
                                 jnp.max(lc, axis=1, keepdims=True))
                pp = jnp.exp(lp - mx)
                pc = jnp.exp(lc - mx)
                den = jnp.sum(pp, axis=1, keepdims=True) + jnp.sum(pc, axis=1, keepdims=True)
                out = (_mm(pp, v_prev) + _mm(pc, v_cur)) / den
                log_den = jnp.broadcast_to(mx + jnp.log(den), (DIL_BLK, HEAD))
                if dil == 1:
                    o_scr[g, pl.ds(start, DIL_BLK), :] = out
                    l_scr[g, pl.ds(start, DIL_BLK), :] = log_den
                else:
                    o_scr[g, pl.ds(start, DIL_BLK, stride=dil), :] = out
                    l_scr[g, pl.ds(start, DIL_BLK, stride=dil), :] = log_den

    l0, l1, l2 = l_scr[0], l_scr[1], l_scr[2]
    mx = jnp.maximum(jnp.maximum(l0, l1), l2)
    w0, w1, w2 = jnp.exp(l0 - mx), jnp.exp(l1 - mx), jnp.exp(l2 - mx)
    mixed = (w0 * o_scr[0] + w1 * o_scr[1] + w2 * o_scr[2]) / (w0 + w1 + w2)
    o_ref[...] = mixed.astype(o_ref.dtype)


def _t5_bucket(dist):
    max_exact = NUM_BUCKETS // 2
    d = jnp.maximum(dist, 1).astype(F32)
    large = max_exact + (jnp.log(d / max_exact) / math.log(MAX_DISTANCE / max_exact)
                         * (NUM_BUCKETS - max_exact)).astype(jnp.int32)
    large = jnp.clip(large, 0, NUM_BUCKETS - 1)
    return jnp.where(dist < max_exact, dist, large)


def _dil_bias_tables(rel_bias):
    n_kv = rel_bias.shape[1] // len(DIL_PAIRS)
    qi = jnp.arange(DIL_BLK)[:, None]
    ki = jnp.arange(2 * DIL_BLK)[None, :]
    steps = jnp.clip(qi + DIL_BLK - ki, 0, DIL_BLK)
    tables = []
    for g, (_, dil) in enumerate(DIL_PAIRS):
        onehot = (_t5_bucket(steps * dil)[:, :, None] == jnp.arange(NUM_BUCKETS)).astype(F32)
        heads = rel_bias.astype(F32)[:, g * n_kv:(g + 1) * n_kv]
        tables.append(jnp.einsum("qkb,bh->hqk", onehot, heads, precision=lax.Precision.HIGHEST))
    return jnp.concatenate(tables, axis=0)


def _dilated(proj, bias_tbl, *, bsz, seq, n_kv, col_q, col_k, col_v):
    ns = seq // DIL_SUPER

    def qspec(g):
        return pl.BlockSpec((DIL_SUPER, HEAD), lambda b, h, j: (b * ns + j, col_q + g * n_kv + h))

    def cur(col):
        return pl.BlockSpec((DIL_SUPER, HEAD), lambda b, h, j: (b * ns + j, col + h))

    def prev(col):
        return pl.BlockSpec((DIL_SUPER, HEAD),
                            lambda b, h, j: (b * ns + jnp.maximum(j - 1, 0), col + h))

    def bspec(g):
        return pl.BlockSpec((None, DIL_BLK, 2 * DIL_BLK), lambda b, h, j: (g * n_kv + h, 0, 0))

    return pl.pallas_call(
        _dil_kernel,
        grid=(bsz, n_kv, ns),
        in_specs=[qspec(0), qspec(1), qspec(2), prev(col_k), cur(col_k), prev(col_v), cur(col_v),
                  bspec(0), bspec(1), bspec(2)],
        out_specs=pl.BlockSpec((DIL_SUPER, HEAD), lambda b, h, j: (b * ns + j, h)),
        out_shape=jax.ShapeDtypeStruct((bsz * seq, n_kv * HEAD), BF16),
        scratch_shapes=[pltpu.VMEM((len(DIL_PAIRS), DIL_SUPER, HEAD), F32)] * 2,
        compiler_params=_params(3),
        name="dilated_attn",
    )(proj, proj, proj, proj, proj, proj, proj, bias_tbl, bias_tbl, bias_tbl)


def _router_kernel(x_ref, g_ref, w_ref, idx_ref, wgt_ref, *, n_experts):
    hn = _rms(x_ref[...], g_ref[...])
    logits = _mm_hi(hn, w_ref[...])
    lane = lax.broadcasted_iota(jnp.int32, logits.shape, 1)
    logits = jnp.where(lane < n_experts, logits, -jnp.inf)
    m1 = jnp.max(logits, axis=1, keepdims=True)
    i1 = jnp.min(jnp.where(logits == m1, lane, HEAD), axis=1, keepdims=True)
    rest = jnp.where(lane == i1, -jnp.inf, logits)
    m2 = jnp.max(rest, axis=1, keepdims=True)
    i2 = jnp.min(jnp.where(rest == m2, lane, HEAD), axis=1, keepdims=True)
    e2 = jnp.exp(m2 - m1)
    idx_ref[...] = jnp.where(lane == 0, i1, i2)
    wgt_ref[...] = jnp.where(lane == 0, 1.0 / (1.0 + e2), e2 / (1.0 + e2))


def _router(x, gain, w_router):
    t, d = x.shape
    n_experts = w_router.shape[1]
    tm = _pick(t, (256, 128, 64, 8))
    w_pad = jnp.zeros((d, HEAD), F32).at[:, :n_experts].set(w_router.astype(F32))
    return pl.pallas_call(
        functools.partial(_router_kernel, n_experts=n_experts),
        grid=(t // tm,),
        in_specs=[pl.BlockSpec((tm, d), lambda i: (i, 0)),
                  pl.BlockSpec((1, d), lambda i: (0, 0)),
                  pl.BlockSpec((d, HEAD), lambda i: (0, 0))],
        out_specs=[pl.BlockSpec((tm, HEAD), lambda i: (i, 0)),
                   pl.BlockSpec((tm, HEAD), lambda i: (i, 0))],
        out_shape=[jax.ShapeDtypeStruct((t, HEAD), jnp.int32),
                   jax.ShapeDtypeStruct((t, HEAD), F32)],
        compiler_params=_params(1),
        name="router",
    )(x, gain.reshape(1, d).astype(F32), w_pad)


def _routing_plan(top_idx, n_experts, tile):
    t = top_idx.shape[0]
    flat = top_idx.reshape(-1)
    onehot = (flat[:, None] == jnp.arange(n_experts)[None, :]).astype(jnp.int32)
    rank = jnp.sum((jnp.cumsum(onehot, axis=0) - onehot) * onehot, axis=1)
    counts = jnp.sum(onehot, axis=0)
    tiles_per = (counts + tile - 1) // tile
    tile_end = jnp.cumsum(tiles_per)
    dest = (tile_end - tiles_per)[flat] * tile + rank
    n_tiles = (TOP_K * t) // tile + n_experts
    tile_ids = jnp.arange(n_tiles)
    tile_expert = jnp.minimum(jnp.sum((tile_ids[:, None] >= tile_end[None, :]).astype(jnp.int32),
                                      axis=1), n_experts - 1).astype(jnp.int32)
    tile_active = (tile_ids < tile_end[-1]).astype(jnp.int32)
    src = jnp.zeros((n_tiles * tile,), jnp.int32).at[dest].set(
        jnp.arange(TOP_K * t, dtype=jnp.int32) // TOP_K)
    return src, dest.reshape(t, TOP_K).astype(jnp.int32), tile_expert, tile_active


def _row_copy(src_hbm, row, dst_vmem, slot, sem):
    return pltpu.make_async_copy(src_hbm.at[pl.ds(row, 1)], dst_vmem.at[pl.ds(slot, 1)], sem)


def _gather_norm_kernel(src_ref, act_ref, h_hbm, gain_ref, o_ref, buf, sem, *, tile):
    i = pl.program_id(0)

    @pl.when(act_ref[i] == 1)
    def _():
        base = i * tile

        def issue(r, carry):
            _row_copy(h_hbm, src_ref[base + r], buf, r, sem).start()
            return carry

        def drain(r, carry):
            _row_copy(h_hbm, 0, buf, r, sem).wait()
            return carry

        lax.fori_loop(0, tile, issue, 0, unroll=8)
        lax.fori_loop(0, tile, drain, 0, unroll=8)
        o_ref[...] = _rms(buf[...], gain_ref[...]).astype(o_ref.dtype)

    @pl.when(act_ref[i] == 0)
    def _():
        o_ref[...] = jnp.zeros_like(o_ref)


def _gather_norm(h, gain, src, tile_active, *, tile):
    d = h.shape[1]
    n_tiles = tile_active.shape[0]
    return pl.pallas_call(
        functools.partial(_gather_norm_kernel, tile=tile),
        grid_spec=pltpu.PrefetchScalarGridSpec(
            num_scalar_prefetch=2,
            grid=(n_tiles,),
            in_specs=[pl.BlockSpec(memory_space=pl.ANY),
                      pl.BlockSpec((1, d), lambda i, s, a: (0, 0))],
            out_specs=pl.BlockSpec((tile, d), lambda i, s, a: (i, 0)),
            scratch_shapes=[pltpu.VMEM((tile, d), F32), pltpu.SemaphoreType.DMA(())]),
        out_shape=jax.ShapeDtypeStruct((n_tiles * tile, d), BF16),
        compiler_params=_params(1),
        name="moe_gather",
    )(src, tile_active, h, gain.reshape(1, d).astype(F32))


def _moe_up_kernel(te_ref, act_ref, a_ref, wg_ref, wu_ref, o_ref):
    i = pl.program_id(1)

    @pl.when(act_ref[i] == 1)
    def _():
        a = a_ref[...]
        g = _dot(a, wg_ref[...].astype(BF16))
        u = _dot(a, wu_ref[...].astype(BF16))
        o_ref[...] = (_silu(g) * u).astype(o_ref.dtype)

    @pl.when(act_ref[i] == 0)
    def _():
        o_ref[...] = jnp.zeros_like(o_ref)


def _moe_up(xg, wg, wu, tile_expert, tile_active, *, tile, tn):
    d = xg.shape[1]
    n = wg.shape[-1]
    n_tiles = tile_active.shape[0]
    w_spec = pl.BlockSpec((None, d, tn), lambda j, i, te, a: (te[i], 0, j))
    return pl.pallas_call(
        _moe_up_kernel,
        grid_spec=pltpu.PrefetchScalarGridSpec(
            num_scalar_prefetch=2,
            grid=(n // tn, n_tiles),
            in_specs=[pl.BlockSpec((tile, d), lambda j, i, te, a: (i, 0)), w_spec, w_spec],
            out_specs=pl.BlockSpec((tile, tn), lambda j, i, te, a: (i, j))),
        out_shape=jax.ShapeDtypeStruct((n_tiles * tile, n), BF16),
        compiler_params=_params(2),
        name="moe_up",
    )(tile_expert, tile_active, xg, wg, wu)


def _moe_down_kernel(te_ref, act_ref, a_ref, w_ref, o_ref):
    i = pl.program_id(1)

    @pl.when(act_ref[i] == 1)
    def _():
        o_ref[...] = _dot(a_ref[...], w_ref[...].astype(BF16))

    @pl.when(act_ref[i] == 0)
    def _():
        o_ref[...] = jnp.zeros_like(o_ref)


def _moe_down(act, wd, tile_expert, tile_active, *, tile, tn):
    kd = act.shape[1]
    n = wd.shape[-1]
    n_tiles = tile_active.shape[0]
    return pl.pallas_call(
        _moe_down_kernel,
        grid_spec=pltpu.PrefetchScalarGridSpec(
            num_scalar_prefetch=2,
            grid=(n // tn, n_tiles),
            in_specs=[pl.BlockSpec((tile, kd), lambda j, i, te, a: (i, 0)),
                      pl.BlockSpec((None, kd, tn), lambda j, i, te, a: (te[i], 0, j))],
            out_specs=pl.BlockSpec((tile, tn), lambda j, i, te, a: (i, j))),
        out_shape=jax.ShapeDtypeStruct((n_tiles * tile, n), F32),
        compiler_params=_params(2),
        name="moe_down",
    )(tile_expert, tile_active, act, wd)


def _combine_kernel(d0_ref, d1_ref, y_hbm, h_ref, w_ref, o_ref, buf0, buf1, sem, *, tc):
    base = pl.program_id(0) * tc

    def issue(r, carry):
        _row_copy(y_hbm, d0_ref[base + r], buf0, r, sem).start()
        _row_copy(y_hbm, d1_ref[base + r], buf1, r, sem).start()
        return carry

    def drain(r, carry):
        _row_copy(y_hbm, 0, buf0, r, sem).wait()
        _row_copy(y_hbm, 0, buf1, r, sem).wait()
        return carry

    lax.fori_loop(0, tc, issue, 0, unroll=8)
    lax.fori_loop(0, tc, drain, 0, unroll=8)
    w = w_ref[...]
    o_ref[...] = h_ref[...] + (w[:, 0:1] * buf0[...] + w[:, 1:2] * buf1[...])


def _combine(h, yg, dest, weights):
    t, d = h.shape
    tc = _pick(t, (256, 128, 64, 8))
    return pl.pallas_call(
        functools.partial(_combine_kernel, tc=tc),
        grid_spec=pltpu.PrefetchScalarGridSpec(
            num_scalar_prefetch=2,
            grid=(t // tc,),
            in_specs=[pl.BlockSpec(memory_space=pl.ANY),
                      pl.BlockSpec((tc, d), lambda i, a, b: (i, 0)),
                      pl.BlockSpec((tc, HEAD), lambda i, a, b: (i, 0))],
            out_specs=pl.BlockSpec((tc, d), lambda i, a, b: (i, 0)),
            scratch_shapes=[pltpu.VMEM((tc, d), F32), pltpu.VMEM((tc, d), F32),
                            pltpu.SemaphoreType.DMA(())]),
        out_shape=jax.ShapeDtypeStruct((t, d), F32),
        compiler_params=_params(1),
        name="moe_combine",
    )(dest[:, 0], dest[:, 1], yg, h, weights)


def _moe(h, gain, w_router, wg, wu, wd):
    t, d = h.shape
    n_experts = w_router.shape[1]
    d_ff = wg.shape[-1]
    tile = _pick(TOP_K * t, (512, 256, 128))
    top_idx, top_w = _router(h, gain, w_router)
    src, dest, tile_expert, tile_active = _routing_plan(top_idx[:, :TOP_K], n_experts, tile)
    xg = _gather_norm(h, gain, src, tile_active, tile=tile)
    act = _moe_up(xg, wg, wu, tile_expert, tile_active, tile=tile, tn=_pick(d_ff, (512, 256, 128)))
    yg = _moe_down(act, wd, tile_expert, tile_active, tile=tile, tn=_pick(d, (512, 256, 128)))
    return _combine(h, yg, dest, top_w)


def _pad_in_proj(w, n_gdn):
    d = w.shape[0]
    split = 4 * n_gdn * HEAD
    main = jnp.concatenate([w[:, :split], w[:, split + 2 * n_gdn:]], axis=1)
    n_main = main.shape[1] // HEAD
    total = -(-(n_main + 1) // 8) * 8
    pad = jnp.zeros((d, total * HEAD - main.shape[1] - 2 * n_gdn), w.dtype)
    return jnp.concatenate([main, w[:, split:split + 2 * n_gdn], pad], axis=1).astype(BF16), n_main


def kernel(x, w_in, conv_gdn, gdn_a_log, gdn_dt_bias, gdn_norm, hgrn_lb, hgrn_norm, rel_bias,
           w_out, norm_mix, norm_ffn, w_gate_dense, w_up_dense, w_down_dense, w_router,
           w_gate_moe, w_up_moe, w_down_moe, norm_final):
    bsz, seq, d = x.shape
    depth = w_in.shape[0]
    t = bsz * seq
    n_gdn = gdn_a_log.shape[1]
    n_hgrn = hgrn_lb.shape[1] // HEAD
    n_q = rel_bias.shape[1]
    n_kv = n_q // len(DIL_PAIRS)

    col_gq = 0
    col_gz = 3 * n_gdn
    col_hq = 4 * n_gdn
    col_dq = col_hq + 4 * n_hgrn
    col_dk = col_dq + n_q
    col_dv = col_dk + n_kv

    bias_tbl = _dil_bias_tables(rel_bias)
    tm = _pick(t, (1024, 512, 256, 128))

    h = x.reshape(t, d).astype(F32)
    for layer in range(depth):
        hn = _rmsnorm(h, norm_mix[layer], BF16)
        w_pad, col_tail = _pad_in_proj(w_in[layer], n_gdn)
        n_in = w_pad.shape[1]
        proj = _matmul(hn, w_pad, out_dtype=F32, tm=tm, tn=_pick(n_in, (1024, 512, 256, 128)),
                       tk=d, name="in_proj")
        oa = _gdn(proj, conv_gdn[layer].astype(F32), gdn_a_log[layer], gdn_dt_bias[layer],
                  gdn_norm[layer], bsz=bsz, seq=seq, n_heads=n_gdn, col_q=col_gq, col_z=col_gz,
                  col_tail=col_tail)
        ob = _hgrn(proj, hgrn_lb, hgrn_norm[layer], bsz=bsz, seq=seq, n_heads=n_hgrn,
                   col_q=col_hq, layer=layer)
        oc = _dilated(proj, bias_tbl, bsz=bsz, seq=seq, n_kv=n_kv, col_q=col_dq, col_k=col_dk,
                      col_v=col_dv)
        h = _out_proj([oa, ob, oc], w_out[layer], h, tm=tm, tn=_pick(d, (512, 256, 128)))

        idx = layer // 2
        if layer % 2 == 0:
            hn = _rmsnorm(h, norm_ffn[layer], BF16)
            d_ff = w_gate_dense.shape[-1]
            act = _swiglu_up(hn, w_gate_dense[idx], w_up_dense[idx], tm=tm,
                             tn=_pick(d_ff, (256, 128)))
            h = _matmul(act, w_down_dense[idx], out_dtype=F32, tm=tm,
                        tn=_pick(d, (1024, 512, 256, 128)),
                        tk=_pick(d_ff, (2048, 1792, 1024, 512, 256, 128)), residual=h,
                        name="ffn_down")
        else:
            h = _moe(h, norm_ffn[layer], w_router[idx], w_gate_moe[idx], w_up_moe[idx],
                     w_down_moe[idx])

    out = _rmsnorm(h, norm_final, x.dtype)
    return out.reshape(bsz, seq, d)
```

```python
import functools
import math

import jax
import jax.numpy as jnp
from jax import lax
from jax.experimental import pallas as pl
from jax.experimental.pallas import tpu as pltpu

F32 = jnp.float32
BF16 = jnp.bfloat16

HEAD = 128
CHUNK = 64
SUB = 8
CONV_WIDTH = 4
DIL_PAIRS = ((128, 1), (512, 4), (2048, 16))
DIL_BLK = 128
DIL_SUPER = 2048
NUM_BUCKETS = 32
MAX_DISTANCE = 2048
TOP_K = 2
RMS_EPS = 1e-6
MASK_VALUE = -1e30
MIN_GATE = 1e-20
VMEM_LIMIT = 56 * 1024 * 1024


def _params(n_axes):
    return pltpu.CompilerParams(dimension_semantics=("arbitrary",) * n_axes,
                                vmem_limit_bytes=VMEM_LIMIT)


def _pick(n, cands):
    for c in cands:
        if n % c == 0:
            return c
    return n


def _dot(a, b):
    return jnp.dot(a, b, preferred_element_type=F32)


def _mm(a, b):
    return _dot(a.astype(BF16), b.astype(BF16))


def _mm_nt(a, b):
    return lax.dot_general(a.astype(BF16), b.astype(BF16), (((1,), (1,)), ((), ())),
                           preferred_element_type=F32)


def _mm_tn(a, b):
    return lax.dot_general(a.astype(BF16), b.astype(BF16), (((0,), (0,)), ((), ())),
                           preferred_element_type=F32)


def _mm_hi(a, b):
    return jnp.dot(a, b, precision=lax.Precision.HIGHEST, preferred_element_type=F32)


def _split(x):
    hi = x.astype(BF16)
    return hi, (x - hi.astype(F32)).astype(BF16)


def _mm_split(a, b):
    return _dot(a[0], b[0]) + (_dot(a[0], b[1]) + _dot(a[1], b[0]))


def _sigmoid(x):
    return 1.0 / (1.0 + jnp.exp(-x))


def _silu(x):
    return x * _sigmoid(x)


def _softplus(x):
    return jnp.maximum(x, 0.0) + jnp.log(1.0 + jnp.exp(-jnp.abs(x)))


def _rms(x, gain):
    return x * lax.rsqrt(jnp.mean(x * x, axis=-1, keepdims=True) + RMS_EPS) * gain


def _chunk_cumsum(x, rin):
    s = 1
    while s < CHUNK:
        x = x + jnp.where(rin >= s, pltpu.roll(x, s, 0), 0.0)
        s *= 2
    return x


def _rmsnorm_kernel(x_ref, g_ref, o_ref):
    o_ref[...] = _rms(x_ref[...], g_ref[...]).astype(o_ref.dtype)


def _rmsnorm(x, gain, out_dtype):
    t, d = x.shape
    tm = _pick(t, (256, 128, 64, 8))
    return pl.pallas_call(
        _rmsnorm_kernel,
        grid=(t // tm,),
        in_specs=[pl.BlockSpec((tm, d), lambda i: (i, 0)),
                  pl.BlockSpec((1, d), lambda i: (0, 0))],
        out_specs=pl.BlockSpec((tm, d), lambda i: (i, 0)),
        out_shape=jax.ShapeDtypeStruct((t, d), out_dtype),
        compiler_params=_params(1),
        name="rmsnorm",
    )(x, gain.reshape(1, d).astype(F32))


def _matmul_kernel(*refs, nk, has_res):
    a_ref, b_ref = refs[0], refs[1]
    r_ref = refs[2] if has_res else None
    o_ref = refs[-1]
    part = _dot(a_ref[...], b_ref[...].astype(BF16))

    def first():
        return part if r_ref is None else r_ref[...] + part

    if nk == 1:
        o_ref[...] = first()
        return
    k = pl.program_id(2)

    @pl.when(k == 0)
    def _():
        o_ref[...] = first()

    @pl.when(k > 0)
    def _():
        o_ref[...] += part


def _matmul(a, b, *, tm, tn, tk, n=None, lead=None, residual=None, name="matmul"):
    m, kd = a.shape
    n = b.shape[-1] if n is None else n
    nk = kd // tk
    if lead is None:
        b_spec = pl.BlockSpec((tk, tn), lambda i, j, k: (k, j))
    else:
        b_spec = pl.BlockSpec((None, tk, tn), lambda i, j, k: (lead, k, j))
    in_specs = [pl.BlockSpec((tm, tk), lambda i, j, k: (i, k)), b_spec]
    args = [a, b]
    if residual is not None:
        in_specs.append(pl.BlockSpec((tm, tn), lambda i, j, k: (i, j),
                                     pipeline_mode=pl.Buffered(1)))
        args.append(residual)
    return pl.pallas_call(
        functools.partial(_matmul_kernel, nk=nk, has_res=residual is not None),
        grid=(m // tm, n // tn, nk),
        in_specs=in_specs,
        out_specs=pl.BlockSpec((tm, tn), lambda i, j, k: (i, j)),
        out_shape=jax.ShapeDtypeStruct((m, n), F32),
        compiler_params=_params(3),
        name=name,
    )(*args)


def _shifted_proj_kernel(a_ref, wm_ref, wx_ref, o_ref, w_scr, *, shift, rows):
    kd, tn = w_scr.shape

    @pl.when(pl.program_id(1) == 0)
    def _():
        def realign(r, carry):
            r0 = pl.multiple_of(r * rows, rows)
            cat = jnp.concatenate([wm_ref[pl.ds(r0, rows), :], wx_ref[pl.ds(r0, rows), :]], axis=1)
            w_scr[pl.ds(r0, rows), :] = pltpu.roll(cat, tn + HEAD - shift, 1)[:, :tn].astype(BF16)
            return carry

        lax.fori_loop(0, kd // rows, realign, 0)

    o_ref[...] = _dot(a_ref[...], w_scr[...])


def _shifted_proj(a, w, lead, *, col0, shift, width, tm, tn):
    m, kd = a.shape
    rows = _pick(kd, (256, 128, 64, 16))
    return pl.pallas_call(
        functools.partial(_shifted_proj_kernel, shift=shift, rows=rows),
        grid=(width // tn, m // tm),
        in_specs=[pl.BlockSpec((tm, kd), lambda j, i: (i, 0)),
                  pl.BlockSpec((None, kd, tn), lambda j, i: (lead, 0, col0 // tn + j)),
                  pl.BlockSpec((None, kd, HEAD),
                               lambda j, i: (lead, 0, (col0 + tn * (j + 1)) // HEAD))],
        out_specs=pl.BlockSpec((tm, tn), lambda j, i: (i, j)),
        out_shape=jax.ShapeDtypeStruct((m, width), F32),
        scratch_shapes=[pltpu.VMEM((kd, tn), BF16)],
        compiler_params=_params(2),
        name="in_proj_shifted",
    )(a, w, w)


def _swiglu_up_kernel(a_ref, wg_ref, wu_ref, o_ref):
    a = a_ref[...]
    g = _dot(a, wg_ref[...].astype(BF16))
    u = _dot(a, wu_ref[...].astype(BF16))
    o_ref[...] = (_silu(g) * u).astype(o_ref.dtype)


def _swiglu_up(a, wg, wu, *, tm, tn):
    m, kd = a.shape
    n = wg.shape[-1]
    w_spec = pl.BlockSpec((kd, tn), lambda i, j: (0, j))
    return pl.pallas_call(
        _swiglu_up_kernel,
        grid=(m // tm, n // tn),
        in_specs=[pl.BlockSpec((tm, kd), lambda i, j: (i, 0)), w_spec, w_spec],
        out_specs=pl.BlockSpec((tm, tn), lambda i, j: (i, j)),
        out_shape=jax.ShapeDtypeStruct((m, n), BF16),
        compiler_params=_params(2),
        name="swiglu_up",
    )(a, wg, wu)


def _out_proj_kernel(*refs, n_pieces):
    a_refs = refs[:n_pieces]
    w_refs = refs[n_pieces:2 * n_pieces]
    r_ref, o_ref = refs[2 * n_pieces], refs[2 * n_pieces + 1]
    acc = r_ref[...]
    for a_ref, w_ref in zip(a_refs, w_refs):
        acc = acc + _dot(a_ref[...], w_ref[...].astype(BF16))
    o_ref[...] = acc


def _out_proj(parts, w, residual, *, tm, tn):
    m = residual.shape[0]
    n = w.shape[1]
    unit = HEAD * functools.reduce(math.gcd, [p.shape[1] // HEAD for p in parts])
    a_specs, a_args = [], []
    for p in parts:
        for c in range(p.shape[1] // unit):
            a_specs.append(pl.BlockSpec((tm, unit), lambda i, j, c=c: (i, c)))
            a_args.append(p)
    n_pieces = len(a_args)
    w_specs = [pl.BlockSpec((unit, tn), lambda i, j, c=c: (c, j)) for c in range(n_pieces)]
    return pl.pallas_call(
        functools.partial(_out_proj_kernel, n_pieces=n_pieces),
        grid=(m // tm, n // tn),
        in_specs=a_specs + w_specs + [pl.BlockSpec((tm, tn), lambda i, j: (i, j))],
        out_specs=pl.BlockSpec((tm, tn), lambda i, j: (i, j)),
        out_shape=jax.ShapeDtypeStruct((m, n), F32),
        compiler_params=_params(2),
        name="out_proj",
    )(*a_args, *([w] * n_pieces), residual)


def _gdn_kernel(alog_ref, dtb_ref, q_ref, qh_ref, k_ref, kh_ref, v_ref, vh_ref, z_ref, t_ref,
                wq_ref, wk_ref, wv_ref, gain_ref, o_ref, s_ref, *, tb, n_heads, group):
    hg = pl.program_id(1)
    first = pl.program_id(2) == 0

    @pl.when(first)
    def _():
        s_ref[...] = jnp.zeros_like(s_ref)

    def conv_silu(x_ref, xh_ref, w_ref):
        halo = jnp.where(first, 0.0, xh_ref[...])
        ext = jnp.concatenate([halo, x_ref[...]], axis=0)
        w = w_ref[...]
        y = pltpu.roll(ext, 3, 0)[SUB:] * w[0:1]
        y = y + pltpu.roll(ext, 2, 0)[SUB:] * w[1:2]
        y = y + pltpu.roll(ext, 1, 0)[SUB:] * w[2:3]
        y = y + ext[SUB:] * w[3:4]
        return _silu(y)

    def l2norm(x):
        return x * lax.rsqrt(jnp.sum(x * x, axis=-1, keepdims=True) + RMS_EPS)

    q_all = conv_silu(q_ref, qh_ref, wq_ref)
    k_all = conv_silu(k_ref, kh_ref, wk_ref)
    v_all = conv_silu(v_ref, vh_ref, wv_ref)
    tail = t_ref[...]
    lane = lax.broadcasted_iota(jnp.int32, (tb, HEAD), 1)
    rin = lax.broadcasted_iota(jnp.int32, (tb, HEAD), 0) & (CHUNK - 1)

    row = lax.broadcasted_iota(jnp.int32, (CHUNK, CHUNK), 0)
    col = lax.broadcasted_iota(jnp.int32, (CHUNK, CHUNK), 1)
    eye = (row == col).astype(F32)
    causal = row >= col
    strict = row > col
    level_masks = [(((row >> k) == (col >> k)) & ((row >> (k - 1)) != (col >> (k - 1)))).astype(F32)
                   for k in range(1, 7)]

    n_chunks = tb // CHUNK
    units = []
    for g in range(group):
        h = hg * group + g
        hs = slice(g * HEAD, (g + 1) * HEAD)
        q_h = l2norm(q_all[:, hs]) * (HEAD ** -0.5)
        k_h = l2norm(k_all[:, hs])
        v_h = v_all[:, hs]
        b_col = jnp.sum(jnp.where(lane == h, tail, 0.0), axis=1, keepdims=True)
        a_col = jnp.sum(jnp.where(lane == h + n_heads, tail, 0.0), axis=1, keepdims=True)
        decay_rate = jnp.exp(jnp.zeros((1, 1), F32) + alog_ref[h])
        log_decay = -decay_rate * _softplus(a_col + dtb_ref[h])
        beta_h = jnp.broadcast_to(_sigmoid(b_col), (tb, HEAD))
        gc_h = _chunk_cumsum(jnp.broadcast_to(log_decay, (tb, HEAD)), rin)
        for c in range(n_chunks):
            sl = slice(c * CHUNK, (c + 1) * CHUNK)
            units.append(dict(q=q_h[sl], k=k_h[sl], v=v_h[sl], beta=beta_h[sl], gc=gc_h[sl]))

    for p in units:
        gcol = p["gc"][:, :CHUNK]
        grow = jnp.sum(gcol * eye, axis=0, keepdims=True)
        p["decay"] = jnp.where(causal, jnp.exp(jnp.where(causal, gcol - grow, 0.0)), 0.0)
        p["egc"] = jnp.exp(p["gc"])
        p["kb"] = p["k"] * p["beta"]
    for p in units:
        p["lower"] = jnp.where(strict, _mm_nt(p["kb"], p["k"]) * p["decay"], 0.0)
        p["attn"] = _mm_nt(p["q"], p["k"]) * p["decay"]
        p["inv"] = eye - p["lower"] * level_masks[0]
    for lvl in range(1, 6):
        for p in units:
            p["inv_s"] = _split(p["inv"])
            p["step"] = _mm_split(p["inv_s"], _split(p["lower"] * level_masks[lvl]))
        for p in units:
            p["inv"] = p["inv"] - _mm_split(_split(p["step"]), p["inv_s"])
    for p in units:
        inv_s = _split(p["inv"])
        p["u"] = _mm_split(inv_s, _split(p["v"] * p["beta"]))
        p["w"] = _mm_split(inv_s, _split(p["kb"] * p["egc"]))
        g_last = p["gc"][CHUNK - 1:CHUNK, :]
        p["q_dec"] = p["q"] * p["egc"]
        p["k_dec"] = p["k"] * jnp.exp(g_last - p["gc"])
        p["e_last"] = jnp.exp(g_last)

    gain = gain_ref[...]
    states = [s_ref[g] for g in range(group)]
    for c in range(n_chunks):
        ps = [units[g * n_chunks + c] for g in range(group)]
        v_new = [p["u"] - _mm(p["w"], s) for p, s in zip(ps, states)]
        outs = [_mm(p["q_dec"], s) + _mm(p["attn"], vn) for p, s, vn in zip(ps, states, v_new)]
        states = [s * p["e_last"] + _mm_tn(p["k_dec"], vn) for p, s, vn in zip(ps, states, v_new)]
        sl = slice(c * CHUNK, (c + 1) * CHUNK)
        for g, o in enumerate(outs):
            hs = slice(g * HEAD, (g + 1) * HEAD)
            o_ref[sl, hs] = (_rms(o, gain) * _silu(z_ref[sl, hs])).astype(o_ref.dtype)
    for g in range(group):
        s_ref[g] = states[g]


def _gdn(proj, conv_w, a_log, dt_bias, gain, *, bsz, seq, n_heads, col_q, col_z, col_tail):
    group = _pick(n_heads, (4, 3, 2))
    tb = _pick(seq, (128, 64))
    nb = seq // tb
    width = group * HEAD

    def main(col):
        return pl.BlockSpec((tb, width), lambda b, h, i: (b * nb + i, col // group + h))

    def halo(col):
        return pl.BlockSpec(
            (SUB, width),
            lambda b, h, i: (jnp.maximum(b * (seq // SUB) + i * (tb // SUB) - 1, 0),
                             col // group + h))

    def wspec(col):
        return pl.BlockSpec((CONV_WIDTH, width), lambda b, h, i: (0, col // group + h))

    smem = pl.BlockSpec(memory_space=pltpu.SMEM)
    col_k = col_q + n_heads
    col_v = col_q + 2 * n_heads
    assert all(c % group == 0 for c in (col_q, col_k, col_v, col_z))
    return pl.pallas_call(
        functools.partial(_gdn_kernel, tb=tb, n_heads=n_heads, group=group),
        grid=(bsz, n_heads // group, nb),
        in_specs=[smem, smem,
                  main(col_q), halo(col_q), main(col_k), halo(col_k), main(col_v), halo(col_v),
                  main(col_z),
                  pl.BlockSpec((tb, HEAD), lambda b, h, i: (b * nb + i, col_tail)),
                  wspec(0), wspec(n_heads), wspec(2 * n_heads),
                  pl.BlockSpec((1, HEAD), lambda b, h, i: (0, 0))],
        out_specs=pl.BlockSpec((tb, width), lambda b, h, i: (b * nb + i, h)),
        out_shape=jax.ShapeDtypeStruct((bsz * seq, n_heads * HEAD), BF16),
        scratch_shapes=[pltpu.VMEM((group, HEAD, HEAD), F32)],
        compiler_params=_params(3),
        name="gdn",
    )(a_log.astype(F32), dt_bias.astype(F32), proj, proj, proj, proj, proj, proj, proj, proj,
      conv_w, conv_w, conv_w, gain.reshape(1, HEAD).astype(F32))


def _hgrn_kernel(q_ref, f_ref, i_ref, g_ref, lb_ref, gain_ref, o_ref, s_ref, *, tb, layer, group):
    @pl.when(pl.program_id(2) == 0)
    def _():
        s_ref[...] = jnp.zeros_like(s_ref)

    lbp = lb_ref[...]
    e = jnp.exp(lbp - jnp.max(lbp, axis=0, keepdims=True))
    probs = e / jnp.sum(e, axis=0, keepdims=True)
    lower = jnp.sum(probs[:layer + 1], axis=0, keepdims=True) - probs[0:1]

    f_pre = f_ref[...]
    f_gate = lower + (1.0 - lower) * _sigmoid(f_pre)
    log_f = jnp.log(jnp.maximum(f_gate, MIN_GATE))
    rin = lax.broadcasted_iota(jnp.int32, f_pre.shape, 0) & (CHUNK - 1)
    b_all = _chunk_cumsum(log_f, rin)
    k_all = (1.0 - lower) * _sigmoid(-f_pre)
    q_all = _silu(q_ref[...])
    v_all = i_ref[...]

    row = lax.broadcasted_iota(jnp.int32, (CHUNK, CHUNK), 0)
    col = lax.broadcasted_iota(jnp.int32, (CHUNK, CHUNK), 1)
    sub_row = lax.broadcasted_iota(jnp.int32, (SUB, 1), 0)
    off_masks = {1 << sh: ((((row >> sh) & 1) == 1) & ((col >> sh) == (row >> sh) - 1)).astype(F32)
                 for sh in (3, 4, 5)}
    gain = gain_ref[...]
    n_sub = CHUNK // SUB
    n_chunks = tb // CHUNK

    units = []
    for g in range(group):
        hs = slice(g * HEAD, (g + 1) * HEAD)
        for c in range(n_chunks):
            sl = slice(c * CHUNK, (c + 1) * CHUNK)
            units.append(dict(q=q_all[sl, hs], k=k_all[sl, hs], b=b_all[sl, hs], v=v_all[sl, hs]))

    for p in units:
        q, k, b = p["q"], p["k"], p["b"]
        b_last = b[CHUNK - 1:CHUNK, :]
        p["q_dec"] = q * jnp.exp(b)
        p["e_last"] = jnp.exp(b_last)
        p["kv"] = _mm_tn(p["v"], k * jnp.exp(b_last - b))
    for m in (8, 16, 32):
        for p in units:
            q, k, b = p["q"], p["k"], p["b"]
            ref_q, ref_k = [], []
            for blk in range(n_sub):
                start = (blk * SUB // m) * m
                ref_q.append(jnp.broadcast_to(b[start:start + 1, :], (SUB, HEAD)))
                nxt = start + m
                if nxt < CHUNK:
                    ref_k.append(jnp.broadcast_to(b[nxt:nxt + 1, :], (SUB, HEAD)))
                else:
                    ref_k.append(b[blk * SUB:(blk + 1) * SUB, :])
            qe = q * jnp.exp(b - jnp.concatenate(ref_q, axis=0))
            ke = k * jnp.exp(jnp.concatenate(ref_k, axis=0) - b)
            part = _mm_nt(qe, ke) * off_masks[m]
            p["scores"] = part if m == 8 else p["scores"] + part
    for p in units:
        p["intra"] = _mm(p["scores"], p["v"])
    for p in units:
        q, k, b, v = p["q"], p["k"], p["b"], p["v"]
        diag = []
        for blk in range(n_sub):
            sl = slice(blk * SUB, (blk + 1) * SUB)
            qi, ki, bi, vi = q[sl], k[sl], b[sl], v[sl]
            acc = jnp.zeros((SUB, HEAD), F32)
            for s in range(SUB):
                m = sub_row >= s
                dec = jnp.where(m, jnp.exp(jnp.where(m, bi - bi[s:s + 1], 0.0)), 0.0)
                wgt = jnp.sum(qi * ki[s:s + 1] * dec, axis=1, keepdims=True)
                acc = acc + wgt * vi[s:s + 1]
            diag.append(acc)
        p["intra"] = p["intra"] + jnp.concatenate(diag, axis=0)

    for g in range(group):
        hs = slice(g * HEAD, (g + 1) * HEAD)
        state_t = s_ref[g]
        for c in range(n_chunks):
            p = units[g * n_chunks + c]
            o = p["intra"] + _mm_nt(p["q_dec"], state_t)
            state_t = state_t * p["e_last"] + p["kv"]
            sl = slice(c * CHUNK, (c + 1) * CHUNK)
            o_ref[sl, hs] = (_rms(o, gain) * _silu(g_ref[sl, hs])).astype(o_ref.dtype)
        s_ref[g] = state_t


def _hgrn(proj, hgrn_lb, gain, *, bsz, seq, n_heads, col_q, layer):
    group = _pick(n_heads, (4, 3, 2))
    tb = _pick(seq, (128, 64))
    nb = seq // tb
    depth = hgrn_lb.shape[0]
    width = group * HEAD
    assert col_q % group == 0

    def main(col):
        return pl.BlockSpec((tb, width), lambda b, h, i: (b * nb + i, col // group + h))

    return pl.pallas_call(
        functools.partial(_hgrn_kernel, tb=tb, layer=layer, group=group),
        grid=(bsz, n_heads // group, nb),
        in_specs=[main(col_q), main(col_q + n_heads), main(col_q + 2 * n_heads),
                  main(col_q + 3 * n_heads),
                  pl.BlockSpec((depth, width), lambda b, h, i: (0, h)),
                  pl.BlockSpec((1, HEAD), lambda b, h, i: (0, 0))],
        out_specs=pl.BlockSpec((tb, width), lambda b, h, i: (b * nb + i, h)),
        out_shape=jax.ShapeDtypeStruct((bsz * seq, n_heads * HEAD), BF16),
        scratch_shapes=[pltpu.VMEM((group, HEAD, HEAD), F32)],
        compiler_params=_params(3),
        name="hgrn2",
    )(proj, proj, proj, proj, hgrn_lb.astype(F32), gain.reshape(1, HEAD).astype(F32))


def _dil_kernel(q0_ref, q1_ref, q2_ref, kp_ref, kc_ref, vp_ref, vc_ref,
                b0_ref, b1_ref, b2_ref, o_ref, o_scr, l_scr):
    first_super = pl.program_id(2) == 0
    q_refs = (q0_ref, q1_ref, q2_ref)
    b_refs = (b0_ref, b1_ref, b2_ref)
    qi = lax.broadcasted_iota(jnp.int32, (DIL_BLK, DIL_BLK), 0)
    ki = lax.broadcasted_iota(jnp.int32, (DIL_BLK, DIL_BLK), 1)
    valid_prev = ki >= qi
    valid_cur = ki <= qi
    scale = HEAD ** -0.5

    def rows(ref, start, dil):
        if dil == 1:
            return ref[pl.ds(start, DIL_BLK), :]
        return ref[pl.ds(start, DIL_BLK, stride=dil), :]

    for g, (window, dil) in enumerate(DIL_PAIRS):
        span = DIL_BLK * dil
        bias_p = b_refs[g][:, :DIL_BLK]
        bias_c = b_refs[g][:, DIL_BLK:]
        for s in range(DIL_SUPER // span):
            for c in range(dil):
                start = s * span + c
                q = rows(q_refs[g], start, dil)
                k_cur = rows(kc_ref, start, dil)
                v_cur = rows(vc_ref, start, dil)
                if s == 0:
                    k_prev = rows(kp_ref, DIL_SUPER - span + c, dil)
                    v_prev = rows(vp_ref, DIL_SUPER - span + c, dil)
                else:
                    k_prev = rows(kc_ref, start - span, dil)
                    v_prev = rows(vc_ref, start - span, dil)
                lp = jnp.where(valid_prev, _mm_nt(q, k_prev) * scale + bias_p, MASK_VALUE)
                if s == 0:
                    lp = jnp.where(first_super, MASK_VALUE, lp)
                lc = jnp.where(valid_cur, _mm_nt(q, k_cur) * scale + bias_c, MASK_VALUE)
                mx = jnp.maximum(jnp.max(lp, axis=1, keepdims=True),
                                 jnp.max(lc, axis=1, keepdims=True))
                pp = jnp.exp(lp - mx)
                pc = jnp.exp(lc - mx)
                den = jnp.sum(pp, axis=1, keepdims=True) + jnp.sum(pc, axis=1, keepdims=True)
                out = (_mm(pp, v_prev) + _mm(pc, v_cur)) / den
                log_den = jnp.broadcast_to(mx + jnp.log(den), (DIL_BLK, HEAD))
                if dil == 1:
                    o_scr[g, pl.ds(start, DIL_BLK), :] = out
                    l_scr[g, pl.ds(start, DIL_BLK), :] = log_den
                else:
                    o_scr[g, pl.ds(start, DIL_BLK, stride=dil), :] = out
                    l_scr[g, pl.ds(start, DIL_BLK, stride=dil), :] = log_den

    l0, l1, l2 = l_scr[0], l_scr[1], l_scr[2]
    mx = jnp.maximum(jnp.maximum(l0, l1), l2)
    w0, w1, w2 = jnp.exp(l0 - mx), jnp.exp(l1 - mx), jnp.exp(l2 - mx)
    mixed = (w0 * o_scr[0] + w1 * o_scr[1] + w2 * o_scr[2]) / (w0 + w1 + w2)
    o_ref[...] = mixed.astype(o_ref.dtype)


def _t5_bucket(dist):
    max_exact = NUM_BUCKETS // 2
    d = jnp.maximum(dist, 1).astype(F32)
    large = max_exact + (jnp.log(d / max_exact) / math.log(MAX_DISTANCE / max_exact)
                         * (NUM_BUCKETS - max_exact)).astype(jnp.int32)
    large = jnp.clip(large, 0, NUM_BUCKETS - 1)
    return jnp.where(dist < max_exact, dist, large)


def _dil_bias_tables(rel_bias):
    n_kv = rel_bias.shape[1] // len(DIL_PAIRS)
    qi = jnp.arange(DIL_BLK)[:, None]
    ki = jnp.arange(2 * DIL_BLK)[None, :]
    steps = jnp.clip(qi + DIL_BLK - ki, 0, DIL_BLK)
    tables = []
    for g, (_, dil) in enumerate(DIL_PAIRS):
        onehot = (_t5_bucket(steps * dil)[:, :, None] == jnp.arange(NUM_BUCKETS)).astype(F32)
        heads = rel_bias.astype(F32)[:, g * n_kv:(g + 1) * n_kv]
        tables.append(jnp.einsum("qkb,bh->hqk", onehot, heads, precision=lax.Precision.HIGHEST))
    return jnp.concatenate(tables, axis=0)


def _dilated(proj, bias_tbl, *, bsz, seq, n_kv, col_q, col_k, col_v):
    ns = seq // DIL_SUPER

    def qspec(g):
        return pl.BlockSpec((DIL_SUPER, HEAD), lambda b, h, j: (b * ns + j, col_q + g * n_kv + h))

    def cur(col):
        return pl.BlockSpec((DIL_SUPER, HEAD), lambda b, h, j: (b * ns + j, col + h))

    def prev(col):
        return pl.BlockSpec((DIL_SUPER, HEAD),
                            lambda b, h, j: (b * ns + jnp.maximum(j - 1, 0), col + h))

    def bspec(g):
        return pl.BlockSpec((None, DIL_BLK, 2 * DIL_BLK), lambda b, h, j: (g * n_kv + h, 0, 0))

    return pl.pallas_call(
        _dil_kernel,
        grid=(bsz, n_kv, ns),
        in_specs=[qspec(0), qspec(1), qspec(2), prev(col_k), cur(col_k), prev(col_v), cur(col_v),
                  bspec(0), bspec(1), bspec(2)],
        out_specs=pl.BlockSpec((DIL_SUPER, HEAD), lambda b, h, j: (b * ns + j, h)),
        out_shape=jax.ShapeDtypeStruct((bsz * seq, n_kv * HEAD), BF16),
        scratch_shapes=[pltpu.VMEM((len(DIL_PAIRS), DIL_SUPER, HEAD), F32)] * 2,
        compiler_params=_params(3),
        name="dilated_attn",
    )(proj, proj, proj, proj, proj, proj, proj, bias_tbl, bias_tbl, bias_tbl)


def _router_kernel(x_ref, g_ref, w_ref, idx_ref, wgt_ref, *, n_experts):
    hn = _rms(x_ref[...], g_ref[...])
    logits = _mm_hi(hn, w_ref[...])
    lane = lax.broadcasted_iota(jnp.int32, logits.shape, 1)
    logits = jnp.where(lane < n_experts, logits, -jnp.inf)
    m1 = jnp.max(logits, axis=1, keepdims=True)
    i1 = jnp.min(jnp.where(logits == m1, lane, HEAD), axis=1, keepdims=True)
    rest = jnp.where(lane == i1, -jnp.inf, logits)
    m2 = jnp.max(rest, axis=1, keepdims=True)
    i2 = jnp.min(jnp.where(rest == m2, lane, HEAD), axis=1, keepdims=True)
    e2 = jnp.exp(m2 - m1)
    idx_ref[...] = jnp.where(lane == 0, i1, i2)
    wgt_ref[...] = jnp.where(lane == 0, 1.0 / (1.0 + e2), e2 / (1.0 + e2))


def _router(x, gain, w_router):
    t, d = x.shape
    n_experts = w_router.shape[1]
    tm = _pick(t, (256, 128, 64, 8))
    w_pad = jnp.zeros((d, HEAD), F32).at[:, :n_experts].set(w_router.astype(F32))
    return pl.pallas_call(
        functools.partial(_router_kernel, n_experts=n_experts),
        grid=(t // tm,),
        in_specs=[pl.BlockSpec((tm, d), lambda i: (i, 0)),
                  pl.BlockSpec((1, d), lambda i: (0, 0)),
                  pl.BlockSpec((d, HEAD), lambda i: (0, 0))],
        out_specs=[pl.BlockSpec((tm, HEAD), lambda i: (i, 0)),
                   pl.BlockSpec((tm, HEAD), lambda i: (i, 0))],
        out_shape=[jax.ShapeDtypeStruct((t, HEAD), jnp.int32),
                   jax.ShapeDtypeStruct((t, HEAD), F32)],
        compiler_params=_params(1),
        name="router",
    )(x, gain.reshape(1, d).astype(F32), w_pad)


def _routing_plan(top_idx, n_experts, tile):
    t = top_idx.shape[0]
    flat = top_idx.reshape(-1)
    onehot = (flat[:, None] == jnp.arange(n_experts)[None, :]).astype(jnp.int32)
    rank = jnp.sum((jnp.cumsum(onehot, axis=0) - onehot) * onehot, axis=1)
    counts = jnp.sum(onehot, axis=0)
    tiles_per = (counts + tile - 1) // tile
    tile_end = jnp.cumsum(tiles_per)
    dest = (tile_end - tiles_per)[flat] * tile + rank
    n_tiles = (TOP_K * t) // tile + n_experts
    tile_ids = jnp.arange(n_tiles)
    tile_expert = jnp.minimum(jnp.sum((tile_ids[:, None] >= tile_end[None, :]).astype(jnp.int32),
                                      axis=1), n_experts - 1).astype(jnp.int32)
    tile_active = (tile_ids < tile_end[-1]).astype(jnp.int32)
    src = jnp.zeros((n_tiles * tile,), jnp.int32).at[dest].set(
        jnp.arange(TOP_K * t, dtype=jnp.int32) // TOP_K)
    return src, dest.reshape(t, TOP_K).astype(jnp.int32), tile_expert, tile_active


def _row_copy(src_hbm, row, dst_vmem, slot, sem):
    return pltpu.make_async_copy(src_hbm.at[pl.ds(row, 1)], dst_vmem.at[pl.ds(slot, 1)], sem)


def _gather_norm_kernel(src_ref, act_ref, h_hbm, gain_ref, o_ref, buf, sem, *, tile):
    i = pl.program_id(0)

    @pl.when(act_ref[i] == 1)
    def _():
        base = i * tile

        def issue(r, carry):
            _row_copy(h_hbm, src_ref[base + r], buf, r, sem).start()
            return carry

        def drain(r, carry):
            _row_copy(h_hbm, 0, buf, r, sem).wait()
            return carry

        lax.fori_loop(0, tile, issue, 0, unroll=8)
        lax.fori_loop(0, tile, drain, 0, unroll=8)
        o_ref[...] = _rms(buf[...], gain_ref[...]).astype(o_ref.dtype)

    @pl.when(act_ref[i] == 0)
    def _():
        o_ref[...] = jnp.zeros_like(o_ref)


def _gather_norm(h, gain, src, tile_active, *, tile):
    d = h.shape[1]
    n_tiles = tile_active.shape[0]
    return pl.pallas_call(
        functools.partial(_gather_norm_kernel, tile=tile),
        grid_spec=pltpu.PrefetchScalarGridSpec(
            num_scalar_prefetch=2,
            grid=(n_tiles,),
            in_specs=[pl.BlockSpec(memory_space=pl.ANY),
                      pl.BlockSpec((1, d), lambda i, s, a: (0, 0))],
            out_specs=pl.BlockSpec((tile, d), lambda i, s, a: (i, 0)),
            scratch_shapes=[pltpu.VMEM((tile, d), F32), pltpu.SemaphoreType.DMA(())]),
        out_shape=jax.ShapeDtypeStruct((n_tiles * tile, d), BF16),
        compiler_params=_params(1),
        name="moe_gather",
    )(src, tile_active, h, gain.reshape(1, d).astype(F32))


def _moe_up_kernel(te_ref, act_ref, a_ref, wg_ref, wu_ref, o_ref):
    i = pl.program_id(1)

    @pl.when(act_ref[i] == 1)
    def _():
        a = a_ref[...]
        g = _dot(a, wg_ref[...].astype(BF16))
        u = _dot(a, wu_ref[...].astype(BF16))
        o_ref[...] = (_silu(g) * u).astype(o_ref.dtype)

    @pl.when(act_ref[i] == 0)
    def _():
        o_ref[...] = jnp.zeros_like(o_ref)


def _moe_up(xg, wg, wu, tile_expert, tile_active, *, tile, tn):
    d = xg.shape[1]
    n = wg.shape[-1]
    n_tiles = tile_active.shape[0]
    w_spec = pl.BlockSpec((None, d, tn), lambda j, i, te, a: (te[i], 0, j))
    return pl.pallas_call(
        _moe_up_kernel,
        grid_spec=pltpu.PrefetchScalarGridSpec(
            num_scalar_prefetch=2,
            grid=(n // tn, n_tiles),
            in_specs=[pl.BlockSpec((tile, d), lambda j, i, te, a: (i, 0)), w_spec, w_spec],
            out_specs=pl.BlockSpec((tile, tn), lambda j, i, te, a: (i, j))),
        out_shape=jax.ShapeDtypeStruct((n_tiles * tile, n), BF16),
        compiler_params=_params(2),
        name="moe_up",
    )(tile_expert, tile_active, xg, wg, wu)


def _moe_down_kernel(te_ref, act_ref, a_ref, w_ref, o_ref):
    i = pl.program_id(1)

    @pl.when(act_ref[i] == 1)
    def _():
        o_ref[...] = _dot(a_ref[...], w_ref[...].astype(BF16))

    @pl.when(act_ref[i] == 0)
    def _():
        o_ref[...] = jnp.zeros_like(o_ref)


def _moe_down(act, wd, tile_expert, tile_active, *, tile, tn):
    kd = act.shape[1]
    n = wd.shape[-1]
    n_tiles = tile_active.shape[0]
    return pl.pallas_call(
        _moe_down_kernel,
        grid_spec=pltpu.PrefetchScalarGridSpec(
            num_scalar_prefetch=2,
            grid=(n // tn, n_tiles),
            in_specs=[pl.BlockSpec((tile, kd), lambda j, i, te, a: (i, 0)),
                      pl.BlockSpec((None, kd, tn), lambda j, i, te, a: (te[i], 0, j))],
            out_specs=pl.BlockSpec((tile, tn), lambda j, i, te, a: (i, j))),
        out_shape=jax.ShapeDtypeStruct((n_tiles * tile, n), F32),
        compiler_params=_params(2),
        name="moe_down",
    )(tile_expert, tile_active, act, wd)


def _combine_kernel(d0_ref, d1_ref, y_hbm, h_ref, w_ref, g_ref, o_ref, buf0, buf1, sem, *, tc,
                    final_norm):
    base = pl.program_id(0) * tc

    def issue(r, carry):
        _row_copy(y_hbm, d0_ref[base + r], buf0, r, sem).start()
        _row_copy(y_hbm, d1_ref[base + r], buf1, r, sem).start()
        return carry

    def drain(r, carry):
        _row_copy(y_hbm, 0, buf0, r, sem).wait()
        _row_copy(y_hbm, 0, buf1, r, sem).wait()
        return carry

    lax.fori_loop(0, tc, issue, 0, unroll=8)
    lax.fori_loop(0, tc, drain, 0, unroll=8)
    w = w_ref[...]
    out = h_ref[...] + (w[:, 0:1] * buf0[...] + w[:, 1:2] * buf1[...])
    if final_norm:
        out = _rms(out, g_ref[...])
    o_ref[...] = out


def _combine(h, yg, dest, weights, final_gain):
    t, d = h.shape
    tc = _pick(t, (256, 128, 64, 8))
    final_norm = final_gain is not None
    gain = (final_gain if final_norm else jnp.ones((d,), F32)).reshape(1, d).astype(F32)
    return pl.pallas_call(
        functools.partial(_combine_kernel, tc=tc, final_norm=final_norm),
        grid_spec=pltpu.PrefetchScalarGridSpec(
            num_scalar_prefetch=2,
            grid=(t // tc,),
            in_specs=[pl.BlockSpec(memory_space=pl.ANY),
                      pl.BlockSpec((tc, d), lambda i, a, b: (i, 0)),
                      pl.BlockSpec((tc, HEAD), lambda i, a, b: (i, 0)),
                      pl.BlockSpec((1, d), lambda i, a, b: (0, 0))],
            out_specs=pl.BlockSpec((tc, d), lambda i, a, b: (i, 0)),
            scratch_shapes=[pltpu.VMEM((tc, d), F32), pltpu.VMEM((tc, d), F32),
                            pltpu.SemaphoreType.DMA(())]),
        out_shape=jax.ShapeDtypeStruct((t, d), F32),
        compiler_params=_params(1),
        name="moe_combine",
    )(dest[:, 0], dest[:, 1], yg, h, weights, gain)


def _moe_tile(rows):
    return 768 if rows >= 6144 else _pick(rows, (256, 128))


def _moe(h, gain, w_router, wg, wu, wd, final_gain):
    t, d = h.shape
    n_experts = w_router.shape[1]
    d_ff = wg.shape[-1]
    tile = _moe_tile(TOP_K * t)
    top_idx, top_w = _router(h, gain, w_router)
    src, dest, tile_expert, tile_active = _routing_plan(top_idx[:, :TOP_K], n_experts, tile)
    xg = _gather_norm(h, gain, src, tile_active, tile=tile)
    act = _moe_up(xg, wg, wu, tile_expert, tile_active, tile=tile, tn=_pick(d_ff, (512, 256, 128)))
    yg = _moe_down(act, wd, tile_expert, tile_active, tile=tile, tn=_pick(d, (512, 256, 128)))
    return _combine(h, yg, dest, top_w, final_gain)


def kernel(x, w_in, conv_gdn, gdn_a_log, gdn_dt_bias, gdn_norm, hgrn_lb, hgrn_norm, rel_bias,
           w_out, norm_mix, norm_ffn, w_gate_dense, w_up_dense, w_down_dense, w_router,
           w_gate_moe, w_up_moe, w_down_moe, norm_final):
    bsz, seq, d = x.shape
    depth = w_in.shape[0]
    t = bsz * seq
    n_gdn = gdn_a_log.shape[1]
    n_hgrn = hgrn_lb.shape[1] // HEAD
    n_q = rel_bias.shape[1]
    n_kv = n_q // len(DIL_PAIRS)

    split = 4 * n_gdn * HEAD
    shift = 2 * n_gdn
    rest = w_in.shape[-1] - split - shift
    tn_a = _pick(split, (512, 256, 128))
    tn_b = _pick(math.gcd(rest, split), (512, 256, 128))
    col_hq = 0
    col_dq = 4 * n_hgrn
    col_dk = col_dq + n_q
    col_dv = col_dk + n_kv

    bias_tbl = _dil_bias_tables(rel_bias)
    tm = _pick(t, (1024, 512, 256, 128))
    final_done = False

    h = x.reshape(t, d).astype(F32)
    for layer in range(depth):
        hn = _rmsnorm(h, norm_mix[layer], BF16)
        proj_a = _matmul(hn, w_in, lead=layer, n=split + tn_a, tm=tm, tn=tn_a, tk=d,
                         name="in_proj")
        proj_b = _shifted_proj(hn, w_in, layer, col0=split, shift=shift, width=rest, tm=tm, tn=tn_b)
        oa = _gdn(proj_a, conv_gdn[layer].astype(F32), gdn_a_log[layer], gdn_dt_bias[layer],
                  gdn_norm[layer], bsz=bsz, seq=seq, n_heads=n_gdn, col_q=0, col_z=3 * n_gdn,
                  col_tail=split // HEAD)
        ob = _hgrn(proj_b, hgrn_lb, hgrn_norm[layer], bsz=bsz, seq=seq, n_heads=n_hgrn,
                   col_q=col_hq, layer=layer)
        oc = _dilated(proj_b, bias_tbl, bsz=bsz, seq=seq, n_kv=n_kv, col_q=col_dq, col_k=col_dk,
                      col_v=col_dv)
        h = _out_proj([oa, ob, oc], w_out[layer], h, tm=tm, tn=_pick(d, (512, 256, 128)))

        idx = layer // 2
        if layer % 2 == 0:
            hn = _rmsnorm(h, norm_ffn[layer], BF16)
            d_ff = w_gate_dense.shape[-1]
            act = _swiglu_up(hn, w_gate_dense[idx], w_up_dense[idx], tm=tm,
                             tn=_pick(d_ff, (256, 128)))
            h = _matmul(act, w_down_dense[idx], tm=tm, tn=_pick(d, (1024, 512, 256, 128)),
                        tk=_pick(d_ff, (2048, 1792, 1024, 512, 256, 128)), residual=h,
                        name="ffn_down")
        else:
            final_done = layer == depth - 1
            h = _moe(h, norm_ffn[layer], w_router[idx], w_gate_moe[idx], w_up_moe[idx],
                     w_down_moe[idx], norm_final if final_done else None)

    out = h if final_done else _rmsnorm(h, norm_final, F32)
    return out.reshape(bsz, seq, d).astype(x.dtype)
```

```python
import functools
import math

import jax
import jax.numpy as jnp
from jax import lax
from jax.experimental import pallas as pl
from jax.experimental.pallas import tpu as pltpu

F32 = jnp.float32
BF16 = jnp.bfloat16

HEAD = 128
CHUNK = 64
SUB = 8
CONV_WIDTH = 4
DIL_PAIRS = ((128, 1), (512, 4), (2048, 16))
DIL_BLK = 128
DIL_SUPER = 2048
NUM_BUCKETS = 32
MAX_DISTANCE = 2048
TOP_K = 2
RMS_EPS = 1e-6
MASK_VALUE = -1e30
MIN_GATE = 1e-20
VMEM_LIMIT = 56 * 1024 * 1024


def _params(n_axes):
    return pltpu.CompilerParams(dimension_semantics=("arbitrary",) * n_axes,
                                vmem_limit_bytes=VMEM_LIMIT)


def _pick(n, cands):
    for c in cands:
        if n % c == 0:
            return c
    return n


def _dot(a, b):
    return jnp.dot(a, b, preferred_element_type=F32)


def _mm(a, b):
    return _dot(a.astype(BF16), b.astype(BF16))


def _mm_nt(a, b):
    return lax.dot_general(a.astype(BF16), b.astype(BF16), (((1,), (1,)), ((), ())),
                           preferred_element_type=F32)


def _mm_tn(a, b):
    return lax.dot_general(a.astype(BF16), b.astype(BF16), (((0,), (0,)), ((), ())),
                           preferred_element_type=F32)


def _mm_hi(a, b):
    return jnp.dot(a, b, precision=lax.Precision.HIGHEST, preferred_element_type=F32)


def _split(x):
    hi = x.astype(BF16)
    return hi, (x - hi.astype(F32)).astype(BF16)


def _mm_split(a, b):
    return _dot(a[0], b[0]) + (_dot(a[0], b[1]) + _dot(a[1], b[0]))


def _sigmoid(x):
    return 1.0 / (1.0 + jnp.exp(-x))


def _silu(x):
    return x * _sigmoid(x)


def _softplus(x):
    return jnp.maximum(x, 0.0) + jnp.log(1.0 + jnp.exp(-jnp.abs(x)))


def _rms(x, gain):
    return x * lax.rsqrt(jnp.mean(x * x, axis=-1, keepdims=True) + RMS_EPS) * gain


def _chunk_cumsum(x, rin):
    s = 1
    while s < CHUNK:
        x = x + jnp.where(rin >= s, pltpu.roll(x, s, 0), 0.0)
        s *= 2
    return x


def _rmsnorm_kernel(x_ref, g_ref, o_ref):
    o_ref[...] = _rms(x_ref[...], g_ref[...]).astype(o_ref.dtype)


def _rmsnorm(x, gain, out_dtype):
    t, d = x.shape
    tm = _pick(t, (256, 128, 64, 8))
    return pl.pallas_call(
        _rmsnorm_kernel,
        grid=(t // tm,),
        in_specs=[pl.BlockSpec((tm, d), lambda i: (i, 0)),
                  pl.BlockSpec((1, d), lambda i: (0, 0))],
        out_specs=pl.BlockSpec((tm, d), lambda i: (i, 0)),
        out_shape=jax.ShapeDtypeStruct((t, d), out_dtype),
        compiler_params=_params(1),
        name="rmsnorm",
    )(x, gain.reshape(1, d).astype(F32))


def _matmul_kernel(*refs, nk, has_res):
    a_ref, b_ref = refs[0], refs[1]
    r_ref = refs[2] if has_res else None
    o_ref = refs[-1]
    part = _dot(a_ref[...], b_ref[...].astype(BF16))

    def first():
        return part if r_ref is None else r_ref[...] + part

    if nk == 1:
        o_ref[...] = first()
        return
    k = pl.program_id(2)

    @pl.when(k == 0)
    def _():
        o_ref[...] = first()

    @pl.when(k > 0)
    def _():
        o_ref[...] += part


def _matmul(a, b, *, tm, tn, tk, residual=None, name="matmul"):
    m, kd = a.shape
    n = b.shape[-1]
    nk = kd // tk
    in_specs = [pl.BlockSpec((tm, tk), lambda i, j, k: (i, k)),
                pl.BlockSpec((tk, tn), lambda i, j, k: (k, j))]
    args = [a, b]
    if residual is not None:
        in_specs.append(pl.BlockSpec((tm, tn), lambda i, j, k: (i, j),
                                     pipeline_mode=pl.Buffered(1)))
        args.append(residual)
    return pl.pallas_call(
        functools.partial(_matmul_kernel, nk=nk, has_res=residual is not None),
        grid=(m // tm, n // tn, nk),
        in_specs=in_specs,
        out_specs=pl.BlockSpec((tm, tn), lambda i, j, k: (i, j)),
        out_shape=jax.ShapeDtypeStruct((m, n), F32),
        compiler_params=_params(3),
        name=name,
    )(*args)


def _proj_t_kernel(*refs, shift, chunk):
    a_ref, wm_ref = refs[0], refs[1]
    wx_ref = refs[2] if shift else None
    o_ref, w_scr = refs[-2], refs[-1]
    kd, tn = w_scr.shape

    @pl.when(pl.program_id(1) == 0)
    def _():
        for r in range(kd // chunk):
            cs = slice(r * chunk, (r + 1) * chunk)
            blk = wm_ref[:, cs]
            if shift:
                cat = jnp.concatenate([blk, wx_ref[:, cs]], axis=0)
                blk = pltpu.roll(cat, tn + HEAD - shift, 0)[:tn]
            w_scr[cs, :] = blk.T.astype(BF16)

    o_ref[...] = _dot(a_ref[...], w_scr[...])


def _proj_t(a, wt, lead, *, row0, shift, width, tm, tn):
    m, kd = a.shape
    in_specs = [pl.BlockSpec((tm, kd), lambda j, i: (i, 0)),
                pl.BlockSpec((None, tn, kd), lambda j, i: (lead, row0 // tn + j, 0))]
    args = [a, wt]
    if shift:
        in_specs.append(pl.BlockSpec((None, HEAD, kd),
                                     lambda j, i: (lead, (row0 + tn * (j + 1)) // HEAD, 0)))
        args.append(wt)
    return pl.pallas_call(
        functools.partial(_proj_t_kernel, shift=shift, chunk=_pick(kd, (512, 256, 128))),
        grid=(width // tn, m // tm),
        in_specs=in_specs,
        out_specs=pl.BlockSpec((tm, tn), lambda j, i: (i, j)),
        out_shape=jax.ShapeDtypeStruct((m, width), F32),
        scratch_shapes=[pltpu.VMEM((kd, tn), BF16)],
        compiler_params=_params(2),
        name="in_proj",
    )(*args)


def _swiglu_up_kernel(a_ref, wg_ref, wu_ref, o_ref, wg_scr, wu_scr):
    @pl.when(pl.program_id(1) == 0)
    def _():
        wg_scr[...] = wg_ref[...].astype(BF16)
        wu_scr[...] = wu_ref[...].astype(BF16)

    a = a_ref[...]
    g = _dot(a, wg_scr[...])
    u = _dot(a, wu_scr[...])
    o_ref[...] = (_silu(g) * u).astype(o_ref.dtype)


def _swiglu_up(a, wg, wu, *, tm, tn):
    m, kd = a.shape
    n = wg.shape[-1]
    w_spec = pl.BlockSpec((kd, tn), lambda j, i: (0, j))
    return pl.pallas_call(
        _swiglu_up_kernel,
        grid=(n // tn, m // tm),
        in_specs=[pl.BlockSpec((tm, kd), lambda j, i: (i, 0)), w_spec, w_spec],
        out_specs=pl.BlockSpec((tm, tn), lambda j, i: (i, j)),
        out_shape=jax.ShapeDtypeStruct((m, n), BF16),
        scratch_shapes=[pltpu.VMEM((kd, tn), BF16)] * 2,
        compiler_params=_params(2),
        name="swiglu_up",
    )(a, wg, wu)


def _out_proj_kernel(*refs, n_pieces):
    a_refs = refs[:n_pieces]
    w_refs = refs[n_pieces:2 * n_pieces]
    r_ref, o_ref = refs[2 * n_pieces], refs[2 * n_pieces + 1]
    acc = r_ref[...]
    for a_ref, w_ref in zip(a_refs, w_refs):
        acc = acc + _dot(a_ref[...], w_ref[...].astype(BF16))
    o_ref[...] = acc


def _out_proj(parts, w, residual, *, tm, tn):
    m = residual.shape[0]
    n = w.shape[1]
    unit = HEAD * functools.reduce(math.gcd, [p.shape[1] // HEAD for p in parts])
    a_specs, a_args = [], []
    for p in parts:
        for c in range(p.shape[1] // unit):
            a_specs.append(pl.BlockSpec((tm, unit), lambda i, j, c=c: (i, c)))
            a_args.append(p)
    n_pieces = len(a_args)
    w_specs = [pl.BlockSpec((unit, tn), lambda i, j, c=c: (c, j)) for c in range(n_pieces)]
    return pl.pallas_call(
        functools.partial(_out_proj_kernel, n_pieces=n_pieces),
        grid=(m // tm, n // tn),
        in_specs=a_specs + w_specs + [pl.BlockSpec((tm, tn), lambda i, j: (i, j))],
        out_specs=pl.BlockSpec((tm, tn), lambda i, j: (i, j)),
        out_shape=jax.ShapeDtypeStruct((m, n), F32),
        compiler_params=_params(2),
        name="out_proj",
    )(*a_args, *([w] * n_pieces), residual)


def _gdn_kernel(alog_ref, dtb_ref, q_ref, qh_ref, k_ref, kh_ref, v_ref, vh_ref, z_ref, t_ref,
                wq_ref, wk_ref, wv_ref, gain_ref, o_ref, s_ref, *, tb, n_heads, group):
    hg = pl.program_id(1)
    first = pl.program_id(2) == 0

    @pl.when(first)
    def _():
        s_ref[...] = jnp.zeros_like(s_ref)

    def conv_silu(x_ref, xh_ref, w_ref):
        halo = jnp.where(first, 0.0, xh_ref[...])
        ext = jnp.concatenate([halo, x_ref[...]], axis=0)
        w = w_ref[...]
        y = pltpu.roll(ext, 3, 0)[SUB:] * w[0:1]
        y = y + pltpu.roll(ext, 2, 0)[SUB:] * w[1:2]
        y = y + pltpu.roll(ext, 1, 0)[SUB:] * w[2:3]
        y = y + ext[SUB:] * w[3:4]
        return _silu(y)

    def l2norm(x):
        return x * lax.rsqrt(jnp.sum(x * x, axis=-1, keepdims=True) + RMS_EPS)

    q_all = conv_silu(q_ref, qh_ref, wq_ref)
    k_all = conv_silu(k_ref, kh_ref, wk_ref)
    v_all = conv_silu(v_ref, vh_ref, wv_ref)
    tail = t_ref[...]
    lane = lax.broadcasted_iota(jnp.int32, (tb, HEAD), 1)
    rin = lax.broadcasted_iota(jnp.int32, (tb, HEAD), 0) & (CHUNK - 1)

    row = lax.broadcasted_iota(jnp.int32, (CHUNK, CHUNK), 0)
    col = lax.broadcasted_iota(jnp.int32, (CHUNK, CHUNK), 1)
    eye = (row == col).astype(F32)
    causal = row >= col
    strict = row > col
    level_masks = [(((row >> k) == (col >> k)) & ((row >> (k - 1)) != (col >> (k - 1)))).astype(F32)
                   for k in range(1, 7)]

    n_chunks = tb // CHUNK
    units = []
    for g in range(group):
        h = hg * group + g
        hs = slice(g * HEAD, (g + 1) * HEAD)
        q_h = l2norm(q_all[:, hs]) * (HEAD ** -0.5)
        k_h = l2norm(k_all[:, hs])
        v_h = v_all[:, hs]
        b_col = jnp.sum(jnp.where(lane == h, tail, 0.0), axis=1, keepdims=True)
        a_col = jnp.sum(jnp.where(lane == h + n_heads, tail, 0.0), axis=1, keepdims=True)
        decay_rate = jnp.exp(jnp.zeros((1, 1), F32) + alog_ref[h])
        log_decay = -decay_rate * _softplus(a_col + dtb_ref[h])
        beta_h = jnp.broadcast_to(_sigmoid(b_col), (tb, HEAD))
        gc_h = _chunk_cumsum(jnp.broadcast_to(log_decay, (tb, HEAD)), rin)
        for c in range(n_chunks):
            sl = slice(c * CHUNK, (c + 1) * CHUNK)
            units.append(dict(q=q_h[sl], k=k_h[sl], v=v_h[sl], beta=beta_h[sl], gc=gc_h[sl]))

    for p in units:
        gcol = p["gc"][:, :CHUNK]
        grow = jnp.sum(gcol * eye, axis=0, keepdims=True)
        p["decay"] = jnp.where(causal, jnp.exp(jnp.where(causal, gcol - grow, 0.0)), 0.0)
        p["egc"] = jnp.exp(p["gc"])
        p["kb"] = p["k"] * p["beta"]
    for p in units:
        p["lower"] = jnp.where(strict, _mm_nt(p["kb"], p["k"]) * p["decay"], 0.0)
        p["attn"] = _mm_nt(p["q"], p["k"]) * p["decay"]
        p["inv"] = eye - p["lower"] * level_masks[0]
    for lvl in range(1, 6):
        for p in units:
            p["inv_s"] = _split(p["inv"])
            p["step"] = _mm_split(p["inv_s"], _split(p["lower"] * level_masks[lvl]))
        for p in units:
            p["inv"] = p["inv"] - _mm_split(_split(p["step"]), p["inv_s"])
    for p in units:
        inv_s = _split(p["inv"])
        p["u"] = _mm_split(inv_s, _split(p["v"] * p["beta"]))
        p["w"] = _mm_split(inv_s, _split(p["kb"] * p["egc"]))
        g_last = p["gc"][CHUNK - 1:CHUNK, :]
        p["q_dec"] = p["q"] * p["egc"]
        p["k_dec"] = p["k"] * jnp.exp(g_last - p["gc"])
        p["e_last"] = jnp.exp(g_last)

    gain = gain_ref[...]
    states = [s_ref[g] for g in range(group)]
    for c in range(n_chunks):
        ps = [units[g * n_chunks + c] for g in range(group)]
        v_new = [p["u"] - _mm(p["w"], s) for p, s in zip(ps, states)]
        outs = [_mm(p["q_dec"], s) + _mm(p["attn"], vn) for p, s, vn in zip(ps, states, v_new)]
        states = [s * p["e_last"] + _mm_tn(p["k_dec"], vn) for p, s, vn in zip(ps, states, v_new)]
        sl = slice(c * CHUNK, (c + 1) * CHUNK)
        for g, o in enumerate(outs):
            hs = slice(g * HEAD, (g + 1) * HEAD)
            o_ref[sl, hs] = (_rms(o, gain) * _silu(z_ref[sl, hs])).astype(o_ref.dtype)
    for g in range(group):
        s_ref[g] = states[g]


def _gdn(proj, conv_w, a_log, dt_bias, gain, *, bsz, seq, n_heads, col_q, col_z, col_tail):
    group = _pick(n_heads, (4, 3, 2))
    tb = _pick(seq, (128, 64))
    nb = seq // tb
    width = group * HEAD

    def main(col):
        return pl.BlockSpec((tb, width), lambda b, h, i: (b * nb + i, col // group + h))

    def halo(col):
        return pl.BlockSpec(
            (SUB, width),
            lambda b, h, i: (jnp.maximum(b * (seq // SUB) + i * (tb // SUB) - 1, 0),
                             col // group + h))

    def wspec(col):
        return pl.BlockSpec((CONV_WIDTH, width), lambda b, h, i: (0, col // group + h))

    smem = pl.BlockSpec(memory_space=pltpu.SMEM)
    col_k = col_q + n_heads
    col_v = col_q + 2 * n_heads
    assert all(c % group == 0 for c in (col_q, col_k, col_v, col_z))
    return pl.pallas_call(
        functools.partial(_gdn_kernel, tb=tb, n_heads=n_heads, group=group),
        grid=(bsz, n_heads // group, nb),
        in_specs=[smem, smem,
                  main(col_q), halo(col_q), main(col_k), halo(col_k), main(col_v), halo(col_v),
                  main(col_z),
                  pl.BlockSpec((tb, HEAD), lambda b, h, i: (b * nb + i, col_tail)),
                  wspec(0), wspec(n_heads), wspec(2 * n_heads),
                  pl.BlockSpec((1, HEAD), lambda b, h, i: (0, 0))],
        out_specs=pl.BlockSpec((tb, width), lambda b, h, i: (b * nb + i, h)),
        out_shape=jax.ShapeDtypeStruct((bsz * seq, n_heads * HEAD), BF16),
        scratch_shapes=[pltpu.VMEM((group, HEAD, HEAD), F32)],
        compiler_params=_params(3),
        name="gdn",
    )(a_log.astype(F32), dt_bias.astype(F32), proj, proj, proj, proj, proj, proj, proj, proj,
      conv_w, conv_w, conv_w, gain.reshape(1, HEAD).astype(F32))


def _hgrn_kernel(q_ref, f_ref, i_ref, g_ref, lb_ref, gain_ref, o_ref, s_ref, *, tb, layer, group):
    @pl.when(pl.program_id(2) == 0)
    def _():
        s_ref[...] = jnp.zeros_like(s_ref)

    lbp = lb_ref[...]
    e = jnp.exp(lbp - jnp.max(lbp, axis=0, keepdims=True))
    probs = e / jnp.sum(e, axis=0, keepdims=True)
    lower = jnp.sum(probs[:layer + 1], axis=0, keepdims=True) - probs[0:1]

    f_pre = f_ref[...]
    f_gate = lower + (1.0 - lower) * _sigmoid(f_pre)
    log_f = jnp.log(jnp.maximum(f_gate, MIN_GATE))
    rin = lax.broadcasted_iota(jnp.int32, f_pre.shape, 0) & (CHUNK - 1)
    b_all = _chunk_cumsum(log_f, rin)
    k_all = (1.0 - lower) * _sigmoid(-f_pre)
    q_all = _silu(q_ref[...])
    v_all = i_ref[...]

    row = lax.broadcasted_iota(jnp.int32, (CHUNK, CHUNK), 0)
    col = lax.broadcasted_iota(jnp.int32, (CHUNK, CHUNK), 1)
    sub_row = lax.broadcasted_iota(jnp.int32, (SUB, 1), 0)
    off_masks = {1 << sh: ((((row >> sh) & 1) == 1) & ((col >> sh) == (row >> sh) - 1)).astype(F32)
                 for sh in (3, 4, 5)}
    gain = gain_ref[...]
    n_sub = CHUNK // SUB
    n_chunks = tb // CHUNK

    units = []
    for g in range(group):
        hs = slice(g * HEAD, (g + 1) * HEAD)
        for c in range(n_chunks):
            sl = slice(c * CHUNK, (c + 1) * CHUNK)
            units.append(dict(q=q_all[sl, hs], k=k_all[sl, hs], b=b_all[sl, hs], v=v_all[sl, hs]))

    for p in units:
        q, k, b = p["q"], p["k"], p["b"]
        b_last = b[CHUNK - 1:CHUNK, :]
        p["q_dec"] = q * jnp.exp(b)
        p["e_last"] = jnp.exp(b_last)
        p["kv"] = _mm_tn(p["v"], k * jnp.exp(b_last - b))
    for m in (8, 16, 32):
        for p in units:
            q, k, b = p["q"], p["k"], p["b"]
            ref_q, ref_k = [], []
            for blk in range(n_sub):
                start = (blk * SUB // m) * m
                ref_q.append(jnp.broadcast_to(b[start:start + 1, :], (SUB, HEAD)))
                nxt = start + m
                if nxt < CHUNK:
                    ref_k.append(jnp.broadcast_to(b[nxt:nxt + 1, :], (SUB, HEAD)))
                else:
                    ref_k.append(b[blk * SUB:(blk + 1) * SUB, :])
            qe = q * jnp.exp(b - jnp.concatenate(ref_q, axis=0))
            ke = k * jnp.exp(jnp.concatenate(ref_k, axis=0) - b)
            part = _mm_nt(qe, ke) * off_masks[m]
            p["scores"] = part if m == 8 else p["scores"] + part
    for p in units:
        p["intra"] = _mm(p["scores"], p["v"])
    for p in units:
        q, k, b, v = p["q"], p["k"], p["b"], p["v"]
        diag = []
        for blk in range(n_sub):
            sl = slice(blk * SUB, (blk + 1) * SUB)
            qi, ki, bi, vi = q[sl], k[sl], b[sl], v[sl]
            acc = jnp.zeros((SUB, HEAD), F32)
            for s in range(SUB):
                m = sub_row >= s
                dec = jnp.where(m, jnp.exp(jnp.where(m, bi - bi[s:s + 1], 0.0)), 0.0)
                wgt = jnp.sum(qi * ki[s:s + 1] * dec, axis=1, keepdims=True)
                acc = acc + wgt * vi[s:s + 1]
            diag.append(acc)
        p["intra"] = p["intra"] + jnp.concatenate(diag, axis=0)

    for g in range(group):
        hs = slice(g * HEAD, (g + 1) * HEAD)
        state_t = s_ref[g]
        for c in range(n_chunks):
            p = units[g * n_chunks + c]
            o = p["intra"] + _mm_nt(p["q_dec"], state_t)
            state_t = state_t * p["e_last"] + p["kv"]
            sl = slice(c * CHUNK, (c + 1) * CHUNK)
            o_ref[sl, hs] = (_rms(o, gain) * _silu(g_ref[sl, hs])).astype(o_ref.dtype)
        s_ref[g] = state_t


def _hgrn(proj, hgrn_lb, gain, *, bsz, seq, n_heads, col_q, layer):
    group = _pick(n_heads, (4, 3, 2))
    tb = _pick(seq, (128, 64))
    nb = seq // tb
    depth = hgrn_lb.shape[0]
    width = group * HEAD
    assert col_q % group == 0

    def main(col):
        return pl.BlockSpec((tb, width), lambda b, h, i: (b * nb + i, col // group + h))

    return pl.pallas_call(
        functools.partial(_hgrn_kernel, tb=tb, layer=layer, group=group),
        grid=(bsz, n_heads // group, nb),
        in_specs=[main(col_q), main(col_q + n_heads), main(col_q + 2 * n_heads),
                  main(col_q + 3 * n_heads),
                  pl.BlockSpec((depth, width), lambda b, h, i: (0, h)),
                  pl.BlockSpec((1, HEAD), lambda b, h, i: (0, 0))],
        out_specs=pl.BlockSpec((tb, width), lambda b, h, i: (b * nb + i, h)),
        out_shape=jax.ShapeDtypeStruct((bsz * seq, n_heads * HEAD), BF16),
        scratch_shapes=[pltpu.VMEM((group, HEAD, HEAD), F32)],
        compiler_params=_params(3),
        name="hgrn2",
    )(proj, proj, proj, proj, hgrn_lb.astype(F32), gain.reshape(1, HEAD).astype(F32))


def _dil_kernel(q0_ref, q1_ref, q2_ref, kp_ref, kc_ref, vp_ref, vc_ref,
                b0_ref, b1_ref, b2_ref, o_ref, o_scr, l_scr):
    first_super = pl.program_id(2) == 0
    q_refs = (q0_ref, q1_ref, q2_ref)
    b_refs = (b0_ref, b1_ref, b2_ref)
    qi = lax.broadcasted_iota(jnp.int32, (DIL_BLK, DIL_BLK), 0)
    ki = lax.broadcasted_iota(jnp.int32, (DIL_BLK, DIL_BLK), 1)
    valid_prev = ki >= qi
    valid_cur = ki <= qi
    scale = HEAD ** -0.5

    def rows(ref, start, dil):
        if dil == 1:
            return ref[pl.ds(start, DIL_BLK), :]
        return ref[pl.ds(start, DIL_BLK, stride=dil), :]

    for g, (window, dil) in enumerate(DIL_PAIRS):
        span = DIL_BLK * dil
        bias_p = b_refs[g][:, :DIL_BLK]
        bias_c = b_refs[g][:, DIL_BLK:]
        for s in range(DIL_SUPER // span):
            for c in range(dil):
                start = s * span + c
                q = rows(q_refs[g], start, dil)
                k_cur = rows(kc_ref, start, dil)
                v_cur = rows(vc_ref, start, dil)
                if s == 0:
                    k_prev = rows(kp_ref, DIL_SUPER - span + c, dil)
                    v_prev = rows(vp_ref, DIL_SUPER - span + c, dil)
                else:
                    k_prev = rows(kc_ref, start - span, dil)
                    v_prev = rows(vc_ref, start - span, dil)
                lp = jnp.where(valid_prev, _mm_nt(q, k_prev) * scale + bias_p, MASK_VALUE)
                if s == 0:
                    lp = jnp.where(first_super, MASK_VALUE, lp)
                lc = jnp.where(valid_cur, _mm_nt(q, k_cur) * scale + bias_c, MASK_VALUE)
                mx = jnp.maximum(jnp.max(lp, axis=1, keepdims=True),
                                 jnp.max(lc, axis=1, keepdims=True))
                pp = jnp.exp(lp - mx)
                pc = jnp.exp(lc - mx)
                den = jnp.sum(pp, axis=1, keepdims=True) + jnp.sum(pc, axis=1, keepdims=True)
                out = (_mm(pp, v_prev) + _mm(pc, v_cur)) / den
                log_den = jnp.broadcast_to(mx + jnp.log(den), (DIL_BLK, HEAD))
                if dil == 1:
                    o_scr[g, pl.ds(start, DIL_BLK), :] = out
                    l_scr[g, pl.ds(start, DIL_BLK), :] = log_den
                else:
                    o_scr[g, pl.ds(start, DIL_BLK, stride=dil), :] = out
                    l_scr[g, pl.ds(start, DIL_BLK, stride=dil), :] = log_den

    l0, l1, l2 = l_scr[0], l_scr[1], l_scr[2]
    mx = jnp.maximum(jnp.maximum(l0, l1), l2)
    w0, w1, w2 = jnp.exp(l0 - mx), jnp.exp(l1 - mx), jnp.exp(l2 - mx)
    mixed = (w0 * o_scr[0] + w1 * o_scr[1] + w2 * o_scr[2]) / (w0 + w1 + w2)
    o_ref[...] = mixed.astype(o_ref.dtype)


def _t5_bucket(dist):
    max_exact = NUM_BUCKETS // 2
    d = jnp.maximum(dist, 1).astype(F32)
    large = max_exact + (jnp.log(d / max_exact) / math.log(MAX_DISTANCE / max_exact)
                         * (NUM_BUCKETS - max_exact)).astype(jnp.int32)
    large = jnp.clip(large, 0, NUM_BUCKETS - 1)
    return jnp.where(dist < max_exact, dist, large)


def _dil_bias_tables(rel_bias):
    n_kv = rel_bias.shape[1] // len(DIL_PAIRS)
    qi = jnp.arange(DIL_BLK)[:, None]
    ki = jnp.arange(2 * DIL_BLK)[None, :]
    steps = jnp.clip(qi + DIL_BLK - ki, 0, DIL_BLK)
    tables = []
    for g, (_, dil) in enumerate(DIL_PAIRS):
        onehot = (_t5_bucket(steps * dil)[:, :, None] == jnp.arange(NUM_BUCKETS)).astype(F32)
        heads = rel_bias.astype(F32)[:, g * n_kv:(g + 1) * n_kv]
        tables.append(jnp.einsum("qkb,bh->hqk", onehot, heads, precision=lax.Precision.HIGHEST))
    return jnp.concatenate(tables, axis=0)


def _dilated(proj, bias_tbl, *, bsz, seq, n_kv, col_q, col_k, col_v):
    ns = seq // DIL_SUPER

    def qspec(g):
        return pl.BlockSpec((DIL_SUPER, HEAD), lambda b, h, j: (b * ns + j, col_q + g * n_kv + h))

    def cur(col):
        return pl.BlockSpec((DIL_SUPER, HEAD), lambda b, h, j: (b * ns + j, col + h))

    def prev(col):
        return pl.BlockSpec((DIL_SUPER, HEAD),
                            lambda b, h, j: (b * ns + jnp.maximum(j - 1, 0), col + h))

    def bspec(g):
        return pl.BlockSpec((None, DIL_BLK, 2 * DIL_BLK), lambda b, h, j: (g * n_kv + h, 0, 0))

    return pl.pallas_call(
        _dil_kernel,
        grid=(bsz, n_kv, ns),
        in_specs=[qspec(0), qspec(1), qspec(2), prev(col_k), cur(col_k), prev(col_v), cur(col_v),
                  bspec(0), bspec(1), bspec(2)],
        out_specs=pl.BlockSpec((DIL_SUPER, HEAD), lambda b, h, j: (b * ns + j, h)),
        out_shape=jax.ShapeDtypeStruct((bsz * seq, n_kv * HEAD), BF16),
        scratch_shapes=[pltpu.VMEM((len(DIL_PAIRS), DIL_SUPER, HEAD), F32)] * 2,
        compiler_params=_params(3),
        name="dilated_attn",
    )(proj, proj, proj, proj, proj, proj, proj, bias_tbl, bias_tbl, bias_tbl)


def _router_kernel(x_ref, g_ref, w_ref, idx_ref, wgt_ref, *, n_experts):
    hn = _rms(x_ref[...], g_ref[...])
    logits = _mm_hi(hn, w_ref[...])
    lane = lax.broadcasted_iota(jnp.int32, logits.shape, 1)
    logits = jnp.where(lane < n_experts, logits, -jnp.inf)
    m1 = jnp.max(logits, axis=1, keepdims=True)
    i1 = jnp.min(jnp.where(logits == m1, lane, HEAD), axis=1, keepdims=True)
    rest = jnp.where(lane == i1, -jnp.inf, logits)
    m2 = jnp.max(rest, axis=1, keepdims=True)
    i2 = jnp.min(jnp.where(rest == m2, lane, HEAD), axis=1, keepdims=True)
    e2 = jnp.exp(m2 - m1)
    idx_ref[...] = jnp.where(lane == 0, i1, i2)
    wgt_ref[...] = jnp.where(lane == 0, 1.0 / (1.0 + e2), e2 / (1.0 + e2))


def _router(x, gain, w_router):
    t, d = x.shape
    n_experts = w_router.shape[1]
    tm = _pick(t, (256, 128, 64, 8))
    w_pad = jnp.zeros((d, HEAD), F32).at[:, :n_experts].set(w_router.astype(F32))
    return pl.pallas_call(
        functools.partial(_router_kernel, n_experts=n_experts),
        grid=(t // tm,),
        in_specs=[pl.BlockSpec((tm, d), lambda i: (i, 0)),
                  pl.BlockSpec((1, d), lambda i: (0, 0)),
                  pl.BlockSpec((d, HEAD), lambda i: (0, 0))],
        out_specs=[pl.BlockSpec((tm, HEAD), lambda i: (i, 0)),
                   pl.BlockSpec((tm, HEAD), lambda i: (i, 0))],
        out_shape=[jax.ShapeDtypeStruct((t, HEAD), jnp.int32),
                   jax.ShapeDtypeStruct((t, HEAD), F32)],
        compiler_params=_params(1),
        name="router",
    )(x, gain.reshape(1, d).astype(F32), w_pad)


def _routing_plan(top_idx, n_experts, tile):
    t = top_idx.shape[0]
    flat = top_idx.reshape(-1)
    onehot = (flat[:, None] == jnp.arange(n_experts)[None, :]).astype(jnp.int32)
    rank = jnp.sum((jnp.cumsum(onehot, axis=0) - onehot) * onehot, axis=1)
    counts = jnp.sum(onehot, axis=0)
    tiles_per = (counts + tile - 1) // tile
    tile_end = jnp.cumsum(tiles_per)
    dest = (tile_end - tiles_per)[flat] * tile + rank
    n_tiles = (TOP_K * t) // tile + n_experts
    tile_ids = jnp.arange(n_tiles)
    tile_expert = jnp.minimum(jnp.sum((tile_ids[:, None] >= tile_end[None, :]).astype(jnp.int32),
                                      axis=1), n_experts - 1).astype(jnp.int32)
    tile_active = (tile_ids < tile_end[-1]).astype(jnp.int32)
    src = jnp.zeros((n_tiles * tile,), jnp.int32).at[dest].set(
        jnp.arange(TOP_K * t, dtype=jnp.int32) // TOP_K)
    return src, dest.reshape(t, TOP_K).astype(jnp.int32), tile_expert, tile_active


def _row_copy(src_hbm, row, dst_vmem, slot, sem):
    return pltpu.make_async_copy(src_hbm.at[pl.ds(row, 1)], dst_vmem.at[pl.ds(slot, 1)], sem)


def _gather_norm_kernel(src_ref, act_ref, h_hbm, gain_ref, o_ref, buf, sem, *, tile):
    i = pl.program_id(0)
    last = pl.num_programs(0) - 1
    slot = i % 2

    def start_tile(t, s):
        @pl.when(act_ref[t] == 1)
        def _():
            def issue(r, carry):
                _row_copy(h_hbm, src_ref[t * tile + r], buf.at[s], r, sem.at[s]).start()
                return carry

            lax.fori_loop(0, tile, issue, 0, unroll=8)

    @pl.when(i == 0)
    def _():
        start_tile(0, 0)

    @pl.when(i < last)
    def _():
        start_tile(i + 1, 1 - slot)

    @pl.when(act_ref[i] == 1)
    def _():
        def drain(r, carry):
            _row_copy(h_hbm, 0, buf.at[slot], r, sem.at[slot]).wait()
            return carry

        lax.fori_loop(0, tile, drain, 0, unroll=8)
        o_ref[...] = _rms(buf[slot], gain_ref[...]).astype(o_ref.dtype)

    @pl.when(act_ref[i] == 0)
    def _():
        o_ref[...] = jnp.zeros_like(o_ref)


def _gather_norm(h, gain, src, tile_active, *, tile):
    d = h.shape[1]
    n_tiles = tile_active.shape[0]
    return pl.pallas_call(
        functools.partial(_gather_norm_kernel, tile=tile),
        grid_spec=pltpu.PrefetchScalarGridSpec(
            num_scalar_prefetch=2,
            grid=(n_tiles,),
            in_specs=[pl.BlockSpec(memory_space=pl.ANY),
                      pl.BlockSpec((1, d), lambda i, s, a: (0, 0))],
            out_specs=pl.BlockSpec((tile, d), lambda i, s, a: (i, 0)),
            scratch_shapes=[pltpu.VMEM((2, tile, d), F32), pltpu.SemaphoreType.DMA((2,))]),
        out_shape=jax.ShapeDtypeStruct((n_tiles * tile, d), BF16),
        compiler_params=_params(1),
        name="moe_gather",
    )(src, tile_active, h, gain.reshape(1, d).astype(F32))


def _moe_up_kernel(te_ref, act_ref, a_ref, wg_ref, wu_ref, o_ref):
    i = pl.program_id(1)

    @pl.when(act_ref[i] == 1)
    def _():
        a = a_ref[...]
        g = _dot(a, wg_ref[...].astype(BF16))
        u = _dot(a, wu_ref[...].astype(BF16))
        o_ref[...] = (_silu(g) * u).astype(o_ref.dtype)

    @pl.when(act_ref[i] == 0)
    def _():
        o_ref[...] = jnp.zeros_like(o_ref)


def _moe_up(xg, wg, wu, tile_expert, tile_active, *, tile, tn):
    d = xg.shape[1]
    n = wg.shape[-1]
    n_tiles = tile_active.shape[0]
    w_spec = pl.BlockSpec((None, d, tn), lambda j, i, te, a: (te[i], 0, j))
    return pl.pallas_call(
        _moe_up_kernel,
        grid_spec=pltpu.PrefetchScalarGridSpec(
            num_scalar_prefetch=2,
            grid=(n // tn, n_tiles),
            in_specs=[pl.BlockSpec((tile, d), lambda j, i, te, a: (i, 0)), w_spec, w_spec],
            out_specs=pl.BlockSpec((tile, tn), lambda j, i, te, a: (i, j))),
        out_shape=jax.ShapeDtypeStruct((n_tiles * tile, n), BF16),
        compiler_params=_params(2),
        name="moe_up",
    )(tile_expert, tile_active, xg, wg, wu)


def _moe_down_kernel(te_ref, act_ref, a_ref, w_ref, o_ref):
    i = pl.program_id(1)

    @pl.when(act_ref[i] == 1)
    def _():
        o_ref[...] = _dot(a_ref[...], w_ref[...].astype(BF16))

    @pl.when(act_ref[i] == 0)
    def _():
        o_ref[...] = jnp.zeros_like(o_ref)


def _moe_down(act, wd, tile_expert, tile_active, *, tile, tn):
    kd = act.shape[1]
    n = wd.shape[-1]
    n_tiles = tile_active.shape[0]
    return pl.pallas_call(
        _moe_down_kernel,
        grid_spec=pltpu.PrefetchScalarGridSpec(
            num_scalar_prefetch=2,
            grid=(n // tn, n_tiles),
            in_specs=[pl.BlockSpec((tile, kd), lambda j, i, te, a: (i, 0)),
                      pl.BlockSpec((None, kd, tn), lambda j, i, te, a: (te[i], 0, j))],
            out_specs=pl.BlockSpec((tile, tn), lambda j, i, te, a: (i, j))),
        out_shape=jax.ShapeDtypeStruct((n_tiles * tile, n), F32),
        compiler_params=_params(2),
        name="moe_down",
    )(tile_expert, tile_active, act, wd)


def _combine_kernel(d0_ref, d1_ref, y_hbm, h_ref, w_ref, g_ref, o_ref, buf0, buf1, sem, *, tc,
                    final_norm):
    i = pl.program_id(0)
    slot = i % 2

    def start_tile(t, s):
        def issue(r, carry):
            _row_copy(y_hbm, d0_ref[t * tc + r], buf0.at[s], r, sem.at[s]).start()
            _row_copy(y_hbm, d1_ref[t * tc + r], buf1.at[s], r, sem.at[s]).start()
            return carry

        lax.fori_loop(0, tc, issue, 0, unroll=8)

    @pl.when(i == 0)
    def _():
        start_tile(0, 0)

    @pl.when(i < pl.num_programs(0) - 1)
    def _():
        start_tile(i + 1, 1 - slot)

    def drain(r, carry):
        _row_copy(y_hbm, 0, buf0.at[slot], r, sem.at[slot]).wait()
        _row_copy(y_hbm, 0, buf1.at[slot], r, sem.at[slot]).wait()
        return carry

    lax.fori_loop(0, tc, drain, 0, unroll=8)
    w = w_ref[...]
    out = h_ref[...] + (w[:, 0:1] * buf0[slot] + w[:, 1:2] * buf1[slot])
    if final_norm:
        out = _rms(out, g_ref[...])
    o_ref[...] = out


def _combine(h, yg, dest, weights, final_gain):
    t, d = h.shape
    tc = _pick(t, (256, 128, 64, 8))
    final_norm = final_gain is not None
    gain = (final_gain if final_norm else jnp.ones((d,), F32)).reshape(1, d).astype(F32)
    return pl.pallas_call(
        functools.partial(_combine_kernel, tc=tc, final_norm=final_norm),
        grid_spec=pltpu.PrefetchScalarGridSpec(
            num_scalar_prefetch=2,
            grid=(t // tc,),
            in_specs=[pl.BlockSpec(memory_space=pl.ANY),
                      pl.BlockSpec((tc, d), lambda i, a, b: (i, 0)),
                      pl.BlockSpec((tc, HEAD), lambda i, a, b: (i, 0)),
                      pl.BlockSpec((1, d), lambda i, a, b: (0, 0))],
            out_specs=pl.BlockSpec((tc, d), lambda i, a, b: (i, 0)),
            scratch_shapes=[pltpu.VMEM((2, tc, d), F32), pltpu.VMEM((2, tc, d), F32),
                            pltpu.SemaphoreType.DMA((2,))]),
        out_shape=jax.ShapeDtypeStruct((t, d), F32),
        compiler_params=_params(1),
        name="moe_combine",
    )(dest[:, 0], dest[:, 1], yg, h, weights, gain)


def _moe_tile(rows):
    return 768 if rows >= 6144 else _pick(rows, (256, 128))


def _moe(h, gain, w_router, wg, wu, wd, final_gain):
    t, d = h.shape
    n_experts = w_router.shape[1]
    d_ff = wg.shape[-1]
    tile = _moe_tile(TOP_K * t)
    top_idx, top_w = _router(h, gain, w_router)
    src, dest, tile_expert, tile_active = _routing_plan(top_idx[:, :TOP_K], n_experts, tile)
    xg = _gather_norm(h, gain, src, tile_active, tile=tile)
    act = _moe_up(xg, wg, wu, tile_expert, tile_active, tile=tile, tn=_pick(d_ff, (512, 256, 128)))
    yg = _moe_down(act, wd, tile_expert, tile_active, tile=tile, tn=_pick(d, (512, 256, 128)))
    return _combine(h, yg, dest, top_w, final_gain)


def kernel(x, w_in, conv_gdn, gdn_a_log, gdn_dt_bias, gdn_norm, hgrn_lb, hgrn_norm, rel_bias,
           w_out, norm_mix, norm_ffn, w_gate_dense, w_up_dense, w_down_dense, w_router,
           w_gate_moe, w_up_moe, w_down_moe, norm_final):
    bsz, seq, d = x.shape
    depth = w_in.shape[0]
    t = bsz * seq
    n_gdn = gdn_a_log.shape[1]
    n_hgrn = hgrn_lb.shape[1] // HEAD
    n_q = rel_bias.shape[1]
    n_kv = n_q // len(DIL_PAIRS)

    split = 4 * n_gdn * HEAD
    shift = 2 * n_gdn
    rest = w_in.shape[-1] - split - shift
    tn_a = _pick(split, (512, 256, 128))
    tn_b = _pick(math.gcd(rest, split), (512, 256, 128))
    col_hq = 0
    col_dq = 4 * n_hgrn
    col_dk = col_dq + n_q
    col_dv = col_dk + n_kv

    w_in_t = jnp.swapaxes(w_in, 1, 2)
    bias_tbl = _dil_bias_tables(rel_bias)
    tm = _pick(t, (1024, 512, 256, 128))
    final_done = False

    h = x.reshape(t, d).astype(F32)
    for layer in range(depth):
        hn = _rmsnorm(h, norm_mix[layer], BF16)
        proj_a = _proj_t(hn, w_in_t, layer, row0=0, shift=0, width=split + tn_a, tm=tm, tn=tn_a)
        proj_b = _proj_t(hn, w_in_t, layer, row0=split, shift=shift, width=rest, tm=tm, tn=tn_b)
        oa = _gdn(proj_a, conv_gdn[layer].astype(F32), gdn_a_log[layer], gdn_dt_bias[layer],
                  gdn_norm[layer], bsz=bsz, seq=seq, n_heads=n_gdn, col_q=0, col_z=3 * n_gdn,
                  col_tail=split // HEAD)
        ob = _hgrn(proj_b, hgrn_lb, hgrn_norm[layer], bsz=bsz, seq=seq, n_heads=n_hgrn,
                   col_q=col_hq, layer=layer)
        oc = _dilated(proj_b, bias_tbl, bsz=bsz, seq=seq, n_kv=n_kv, col_q=col_dq, col_k=col_dk,
                      col_v=col_dv)
        h = _out_proj([oa, ob, oc], w_out[layer], h, tm=tm, tn=_pick(d, (512, 256, 128)))

        idx = layer // 2
        if layer % 2 == 0:
            hn = _rmsnorm(h, norm_ffn[layer], BF16)
            d_ff = w_gate_dense.shape[-1]
            act = _swiglu_up(hn, w_gate_dense[idx], w_up_dense[idx], tm=tm,
                             tn=_pick(d_ff, (256, 128)))
            h = _matmul(act, w_down_dense[idx], tm=tm, tn=_pick(d, (1024, 512, 256, 128)),
                        tk=_pick(d_ff, (2048, 1792, 1024, 512, 256, 128)), residual=h,
                        name="ffn_down")
        else:
            final_done = layer == depth - 1
            h = _moe(h, norm_ffn[layer], w_router[idx], w_gate_moe[idx], w_up_moe[idx],
                     w_down_moe[idx], norm_final if final_done else None)

    out = h if final_done else _rmsnorm(h, norm_final, F32)
    return out.reshape(bsz, seq, d).astype(x.dtype)
```

```python
import functools
import math

import jax
import jax.numpy as jnp
from jax import lax
from jax.experimental import pallas as pl
from jax.experimental.pallas import tpu as pltpu

F32 = jnp.float32
BF16 = jnp.bfloat16

HEAD = 128
CHUNK = 64
SUB = 8
CONV_WIDTH = 4
DIL_PAIRS = ((128, 1), (512, 4), (2048, 16))
DIL_BLK = 128
DIL_SUPER = 2048
NUM_BUCKETS = 32
MAX_DISTANCE = 2048
TOP_K = 2
RMS_EPS = 1e-6
MASK_VALUE = -1e30
MIN_GATE = 1e-20
VMEM_LIMIT = 56 * 1024 * 1024


def _params(n_axes):
    return pltpu.CompilerParams(dimension_semantics=("arbitrary",) * n_axes,
                                vmem_limit_bytes=VMEM_LIMIT)


def _pick(n, cands):
    for c in cands:
        if n % c == 0:
            return c
    return n


def _dot(a, b):
    return jnp.dot(a, b, preferred_element_type=F32)


def _mm(a, b):
    return _dot(a.astype(BF16), b.astype(BF16))


def _mm_nt(a, b):
    return lax.dot_general(a.astype(BF16), b.astype(BF16), (((1,), (1,)), ((), ())),
                           preferred_element_type=F32)


def _mm_tn(a, b):
    return lax.dot_general(a.astype(BF16), b.astype(BF16), (((0,), (0,)), ((), ())),
                           preferred_element_type=F32)


def _split(x):
    hi = x.astype(BF16)
    return hi, (x - hi.astype(F32)).astype(BF16)


def _mm_split(a, b):
    return _dot(a[0], b[0]) + (_dot(a[0], b[1]) + _dot(a[1], b[0]))


def _sigmoid(x):
    return 1.0 / (1.0 + jnp.exp(-x))


def _silu(x):
    return x * _sigmoid(x)


def _softplus(x):
    return jnp.maximum(x, 0.0) + jnp.log(1.0 + jnp.exp(-jnp.abs(x)))


def _rms(x, gain):
    return x * lax.rsqrt(jnp.mean(x * x, axis=-1, keepdims=True) + RMS_EPS) * gain


def _chunk_cumsum(x, rin):
    s = 1
    while s < CHUNK:
        x = x + jnp.where(rin >= s, pltpu.roll(x, s, 0), 0.0)
        s *= 2
    return x


def _rmsnorm_kernel(x_ref, g_ref, o_ref):
    o_ref[...] = _rms(x_ref[...], g_ref[...]).astype(o_ref.dtype)


def _rmsnorm(x, gain, out_dtype):
    t, d = x.shape
    tm = _pick(t, (256, 128, 64, 8))
    return pl.pallas_call(
        _rmsnorm_kernel,
        grid=(t // tm,),
        in_specs=[pl.BlockSpec((tm, d), lambda i: (i, 0)),
                  pl.BlockSpec((1, d), lambda i: (0, 0))],
        out_specs=pl.BlockSpec((tm, d), lambda i: (i, 0)),
        out_shape=jax.ShapeDtypeStruct((t, d), out_dtype),
        compiler_params=_params(1),
        name="rmsnorm",
    )(x, gain.reshape(1, d).astype(F32))


def _matmul_kernel(*refs, nk, has_res):
    a_ref, b_ref = refs[0], refs[1]
    r_ref = refs[2] if has_res else None
    o_ref = refs[-1]
    part = _dot(a_ref[...], b_ref[...].astype(BF16))

    def first():
        return part if r_ref is None else r_ref[...] + part

    if nk == 1:
        o_ref[...] = first()
        return
    @pl.when(pl.program_id(2) == 0)
    def _():
        o_ref[...] = r_ref[...] if r_ref is not None else jnp.zeros_like(o_ref)

    o_ref[...] += part


def _matmul(a, b, *, tm, tn, tk, residual=None, name="matmul"):
    m, kd = a.shape
    n = b.shape[-1]
    nk = kd // tk
    in_specs = [pl.BlockSpec((tm, tk), lambda i, j, k: (i, k)),
                pl.BlockSpec((tk, tn), lambda i, j, k: (k, j))]
    args = [a, b]
    if residual is not None:
        in_specs.append(pl.BlockSpec((tm, tn), lambda i, j, k: (i, j),
                                     pipeline_mode=pl.Buffered(1)))
        args.append(residual)
    return pl.pallas_call(
        functools.partial(_matmul_kernel, nk=nk, has_res=residual is not None),
        grid=(m // tm, n // tn, nk),
        in_specs=in_specs,
        out_specs=pl.BlockSpec((tm, tn), lambda i, j, k: (i, j)),
        out_shape=jax.ShapeDtypeStruct((m, n), F32),
        compiler_params=_params(3),
        name=name,
    )(*args)


def _proj_t_kernel(*refs, shift, chunk):
    a_ref, wm_ref = refs[0], refs[1]
    wx_ref = refs[2] if shift else None
    o_ref, w_scr = refs[-2], refs[-1]
    kd, tn = w_scr.shape

    @pl.when(pl.program_id(1) == 0)
    def _():
        for r in range(kd // chunk):
            cs = slice(r * chunk, (r + 1) * chunk)
            blk = wm_ref[:, cs]
            if shift:
                cat = jnp.concatenate([blk, wx_ref[:, cs]], axis=0)
                blk = pltpu.roll(cat, tn + HEAD - shift, 0)[:tn]
            w_scr[cs, :] = blk.T.astype(BF16)

    o_ref[...] = _dot(a_ref[...], w_scr[...])


def _proj_t(a, wt, lead, *, row0, shift, width, tm, tn):
    m, kd = a.shape
    in_specs = [pl.BlockSpec((tm, kd), lambda j, i: (i, 0)),
                pl.BlockSpec((None, tn, kd), lambda j, i: (lead, row0 // tn + j, 0))]
    args = [a, wt]
    if shift:
        in_specs.append(pl.BlockSpec((None, HEAD, kd),
                                     lambda j, i: (lead, (row0 + tn * (j + 1)) // HEAD, 0)))
        args.append(wt)
    return pl.pallas_call(
        functools.partial(_proj_t_kernel, shift=shift, chunk=_pick(kd, (512, 256, 128))),
        grid=(width // tn, m // tm),
        in_specs=in_specs,
        out_specs=pl.BlockSpec((tm, tn), lambda j, i: (i, j)),
        out_shape=jax.ShapeDtypeStruct((m, width), F32),
        scratch_shapes=[pltpu.VMEM((kd, tn), BF16)],
        compiler_params=_params(2),
        name="in_proj",
    )(*args)


def _swiglu_up_kernel(a_ref, wg_ref, wu_ref, o_ref):
    a = a_ref[...]
    g = _dot(a, wg_ref[...].astype(BF16))
    u = _dot(a, wu_ref[...].astype(BF16))
    o_ref[...] = (_silu(g) * u).astype(o_ref.dtype)


def _swiglu_up(a, wg, wu, *, tm, tn):
    m, kd = a.shape
    n = wg.shape[-1]
    w_spec = pl.BlockSpec((kd, tn), lambda i, j: (0, j))
    return pl.pallas_call(
        _swiglu_up_kernel,
        grid=(m // tm, n // tn),
        in_specs=[pl.BlockSpec((tm, kd), lambda i, j: (i, 0)), w_spec, w_spec],
        out_specs=pl.BlockSpec((tm, tn), lambda i, j: (i, j)),
        out_shape=jax.ShapeDtypeStruct((m, n), BF16),
        compiler_params=_params(2),
        name="swiglu_up",
    )(a, wg, wu)


def _out_proj_kernel(*refs, n_pieces):
    a_refs = refs[:n_pieces]
    w_refs = refs[n_pieces:2 * n_pieces]
    r_ref, o_ref = refs[2 * n_pieces], refs[2 * n_pieces + 1]
    acc = r_ref[...]
    for a_ref, w_ref in zip(a_refs, w_refs):
        acc = acc + _dot(a_ref[...], w_ref[...].astype(BF16))
    o_ref[...] = acc


def _out_proj(parts, w, residual, *, tm, tn):
    m = residual.shape[0]
    n = w.shape[1]
    unit = HEAD * functools.reduce(math.gcd, [p.shape[1] // HEAD for p in parts])
    a_specs, a_args = [], []
    for p in parts:
        for c in range(p.shape[1] // unit):
            a_specs.append(pl.BlockSpec((tm, unit), lambda i, j, c=c: (i, c)))
            a_args.append(p)
    n_pieces = len(a_args)
    w_specs = [pl.BlockSpec((unit, tn), lambda i, j, c=c: (c, j)) for c in range(n_pieces)]
    return pl.pallas_call(
        functools.partial(_out_proj_kernel, n_pieces=n_pieces),
        grid=(m // tm, n // tn),
        in_specs=a_specs + w_specs + [pl.BlockSpec((tm, tn), lambda i, j: (i, j))],
        out_specs=pl.BlockSpec((tm, tn), lambda i, j: (i, j)),
        out_shape=jax.ShapeDtypeStruct((m, n), F32),
        compiler_params=_params(2),
        name="out_proj",
    )(*a_args, *([w] * n_pieces), residual)


def _gdn_kernel(alog_ref, dtb_ref, q_ref, qh_ref, k_ref, kh_ref, v_ref, vh_ref, z_ref, t_ref,
                wq_ref, wk_ref, wv_ref, gain_ref, o_ref, s_ref, *, tb, n_heads, group):
    hg = pl.program_id(1)
    first = pl.program_id(2) == 0

    @pl.when(first)
    def _():
        s_ref[...] = jnp.zeros_like(s_ref)

    def conv_silu(x_ref, xh_ref, w_ref):
        halo = jnp.where(first, 0.0, xh_ref[...])
        ext = jnp.concatenate([halo, x_ref[...]], axis=0)
        w = w_ref[...]
        y = pltpu.roll(ext, 3, 0)[SUB:] * w[0:1]
        y = y + pltpu.roll(ext, 2, 0)[SUB:] * w[1:2]
        y = y + pltpu.roll(ext, 1, 0)[SUB:] * w[2:3]
        y = y + ext[SUB:] * w[3:4]
        return _silu(y)

    def l2norm(x):
        return x * lax.rsqrt(jnp.sum(x * x, axis=-1, keepdims=True) + RMS_EPS)

    q_all = conv_silu(q_ref, qh_ref, wq_ref)
    k_all = conv_silu(k_ref, kh_ref, wk_ref)
    v_all = conv_silu(v_ref, vh_ref, wv_ref)
    tail = t_ref[...]
    lane = lax.broadcasted_iota(jnp.int32, (tb, HEAD), 1)
    rin = lax.broadcasted_iota(jnp.int32, (tb, HEAD), 0) & (CHUNK - 1)

    row = lax.broadcasted_iota(jnp.int32, (CHUNK, CHUNK), 0)
    col = lax.broadcasted_iota(jnp.int32, (CHUNK, CHUNK), 1)
    eye = (row == col).astype(F32)
    causal = row >= col
    strict = row > col
    level_masks = [(((row >> k) == (col >> k)) & ((row >> (k - 1)) != (col >> (k - 1)))).astype(F32)
                   for k in range(1, 7)]

    n_chunks = tb // CHUNK
    units = []
    for g in range(group):
        h = hg * group + g
        hs = slice(g * HEAD, (g + 1) * HEAD)
        q_h = l2norm(q_all[:, hs]) * (HEAD ** -0.5)
        k_h = l2norm(k_all[:, hs])
        v_h = v_all[:, hs]
        b_col = jnp.sum(jnp.where(lane == h, tail, 0.0), axis=1, keepdims=True)
        a_col = jnp.sum(jnp.where(lane == h + n_heads, tail, 0.0), axis=1, keepdims=True)
        decay_rate = jnp.exp(jnp.zeros((1, 1), F32) + alog_ref[h])
        log_decay = -decay_rate * _softplus(a_col + dtb_ref[h])
        beta_h = jnp.broadcast_to(_sigmoid(b_col), (tb, HEAD))
        gc_h = _chunk_cumsum(jnp.broadcast_to(log_decay, (tb, HEAD)), rin)
        for c in range(n_chunks):
            sl = slice(c * CHUNK, (c + 1) * CHUNK)
            units.append(dict(q=q_h[sl], k=k_h[sl], v=v_h[sl], beta=beta_h[sl], gc=gc_h[sl]))

    for p in units:
        gcol = p["gc"][:, :CHUNK]
        grow = jnp.sum(gcol * eye, axis=0, keepdims=True)
        p["decay"] = jnp.where(causal, jnp.exp(jnp.where(causal, gcol - grow, 0.0)), 0.0)
        p["egc"] = jnp.exp(p["gc"])
        p["kb"] = p["k"] * p["beta"]
    for p in units:
        p["lower"] = jnp.where(strict, _mm_nt(p["kb"], p["k"]) * p["decay"], 0.0)
        p["attn"] = _mm_nt(p["q"], p["k"]) * p["decay"]
        p["inv"] = eye - p["lower"] * level_masks[0]
    for lvl in range(1, 6):
        for p in units:
            p["inv_s"] = _split(p["inv"])
            p["step"] = _mm_split(p["inv_s"], _split(p["lower"] * level_masks[lvl]))
        for p in units:
            p["inv"] = p["inv"] - _mm_split(_split(p["step"]), p["inv_s"])
    for p in units:
        inv_s = _split(p["inv"])
        p["u"] = _mm_split(inv_s, _split(p["v"] * p["beta"]))
        p["w"] = _mm_split(inv_s, _split(p["kb"] * p["egc"]))
        g_last = p["gc"][CHUNK - 1:CHUNK, :]
        p["q_dec"] = p["q"] * p["egc"]
        p["k_dec"] = p["k"] * jnp.exp(g_last - p["gc"])
        p["e_last"] = jnp.exp(g_last)

    gain = gain_ref[...]
    states = [s_ref[g] for g in range(group)]
    for c in range(n_chunks):
        ps = [units[g * n_chunks + c] for g in range(group)]
        v_new = [p["u"] - _mm(p["w"], s) for p, s in zip(ps, states)]
        outs = [_mm(p["q_dec"], s) + _mm(p["attn"], vn) for p, s, vn in zip(ps, states, v_new)]
        states = [s * p["e_last"] + _mm_tn(p["k_dec"], vn) for p, s, vn in zip(ps, states, v_new)]
        sl = slice(c * CHUNK, (c + 1) * CHUNK)
        for g, o in enumerate(outs):
            hs = slice(g * HEAD, (g + 1) * HEAD)
            o_ref[sl, hs] = (_rms(o, gain) * _silu(z_ref[sl, hs])).astype(o_ref.dtype)
    for g in range(group):
        s_ref[g] = states[g]


def _gdn(proj, conv_w, a_log, dt_bias, gain, *, bsz, seq, n_heads, col_q, col_z, col_tail):
    group = _pick(n_heads, (4, 3, 2))
    tb = _pick(seq, (256, 128, 64))
    nb = seq // tb
    width = group * HEAD

    def main(col):
        return pl.BlockSpec((tb, width), lambda b, h, i: (b * nb + i, col // group + h))

    def halo(col):
        return pl.BlockSpec(
            (SUB, width),
            lambda b, h, i: (jnp.maximum(b * (seq // SUB) + i * (tb // SUB) - 1, 0),
                             col // group + h))

    def wspec(col):
        return pl.BlockSpec((CONV_WIDTH, width), lambda b, h, i: (0, col // group + h))

    smem = pl.BlockSpec(memory_space=pltpu.SMEM)
    col_k = col_q + n_heads
    col_v = col_q + 2 * n_heads
    assert all(c % group == 0 for c in (col_q, col_k, col_v, col_z))
    return pl.pallas_call(
        functools.partial(_gdn_kernel, tb=tb, n_heads=n_heads, group=group),
        grid=(bsz, n_heads // group, nb),
        in_specs=[smem, smem,
                  main(col_q), halo(col_q), main(col_k), halo(col_k), main(col_v), halo(col_v),
                  main(col_z),
                  pl.BlockSpec((tb, HEAD), lambda b, h, i: (b * nb + i, col_tail)),
                  wspec(0), wspec(n_heads), wspec(2 * n_heads),
                  pl.BlockSpec((1, HEAD), lambda b, h, i: (0, 0))],
        out_specs=pl.BlockSpec((tb, width), lambda b, h, i: (b * nb + i, h)),
        out_shape=jax.ShapeDtypeStruct((bsz * seq, n_heads * HEAD), BF16),
        scratch_shapes=[pltpu.VMEM((group, HEAD, HEAD), F32)],
        compiler_params=_params(3),
        name="gdn",
    )(a_log.astype(F32), dt_bias.astype(F32), proj, proj, proj, proj, proj, proj, proj, proj,
      conv_w, conv_w, conv_w, gain.reshape(1, HEAD).astype(F32))


def _hgrn_kernel(q_ref, f_ref, i_ref, g_ref, lb_ref, gain_ref, o_ref, s_ref, *, tb, layer, group):
    @pl.when(pl.program_id(2) == 0)
    def _():
        s_ref[...] = jnp.zeros_like(s_ref)

    lbp = lb_ref[...]
    e = jnp.exp(lbp - jnp.max(lbp, axis=0, keepdims=True))
    probs = e / jnp.sum(e, axis=0, keepdims=True)
    lower = jnp.sum(probs[:layer + 1], axis=0, keepdims=True) - probs[0:1]

    f_pre = f_ref[...]
    f_gate = lower + (1.0 - lower) * _sigmoid(f_pre)
    log_f = jnp.log(jnp.maximum(f_gate, MIN_GATE))
    rin = lax.broadcasted_iota(jnp.int32, f_pre.shape, 0) & (CHUNK - 1)
    b_all = _chunk_cumsum(log_f, rin)
    k_all = (1.0 - lower) * _sigmoid(-f_pre)
    q_all = _silu(q_ref[...])
    v_all = i_ref[...]

    row = lax.broadcasted_iota(jnp.int32, (CHUNK, CHUNK), 0)
    col = lax.broadcasted_iota(jnp.int32, (CHUNK, CHUNK), 1)
    sub_row = lax.broadcasted_iota(jnp.int32, (SUB, 1), 0)
    off_masks = {1 << sh: ((((row >> sh) & 1) == 1) & ((col >> sh) == (row >> sh) - 1)).astype(F32)
                 for sh in (3, 4, 5)}
    gain = gain_ref[...]
    n_sub = CHUNK // SUB
    n_chunks = tb // CHUNK

    units = []
    for g in range(group):
        hs = slice(g * HEAD, (g + 1) * HEAD)
        for c in range(n_chunks):
            sl = slice(c * CHUNK, (c + 1) * CHUNK)
            units.append(dict(q=q_all[sl, hs], k=k_all[sl, hs], b=b_all[sl, hs], v=v_all[sl, hs]))

    for p in units:
        q, k, b = p["q"], p["k"], p["b"]
        b_last = b[CHUNK - 1:CHUNK, :]
        p["q_dec"] = q * jnp.exp(b)
        p["e_last"] = jnp.exp(b_last)
        p["kv"] = _mm_tn(p["v"], k * jnp.exp(b_last - b))
    for m in (8, 16, 32):
        for p in units:
            q, k, b = p["q"], p["k"], p["b"]
            ref_q, ref_k = [], []
            for blk in range(n_sub):
                start = (blk * SUB // m) * m
                ref_q.append(jnp.broadcast_to(b[start:start + 1, :], (SUB, HEAD)))
                nxt = start + m
                if nxt < CHUNK:
                    ref_k.append(jnp.broadcast_to(b[nxt:nxt + 1, :], (SUB, HEAD)))
                else:
                    ref_k.append(b[blk * SUB:(blk + 1) * SUB, :])
            qe = q * jnp.exp(b - jnp.concatenate(ref_q, axis=0))
            ke = k * jnp.exp(jnp.concatenate(ref_k, axis=0) - b)
            part = _mm_nt(qe, ke) * off_masks[m]
            p["scores"] = part if m == 8 else p["scores"] + part
    for p in units:
        p["intra"] = _mm(p["scores"], p["v"])
    for p in units:
        q, k, b, v = p["q"], p["k"], p["b"], p["v"]
        diag = []
        for blk in range(n_sub):
            sl = slice(blk * SUB, (blk + 1) * SUB)
            qi, ki, bi, vi = q[sl], k[sl], b[sl], v[sl]
            acc = jnp.zeros((SUB, HEAD), F32)
            for s in range(SUB):
                m = sub_row >= s
                dec = jnp.where(m, jnp.exp(jnp.where(m, bi - bi[s:s + 1], 0.0)), 0.0)
                wgt = jnp.sum(qi * ki[s:s + 1] * dec, axis=1, keepdims=True)
                acc = acc + wgt * vi[s:s + 1]
            diag.append(acc)
        p["intra"] = p["intra"] + jnp.concatenate(diag, axis=0)

    for g in range(group):
        hs = slice(g * HEAD, (g + 1) * HEAD)
        state_t = s_ref[g]
        for c in range(n_chunks):
            p = units[g * n_chunks + c]
            o = p["intra"] + _mm_nt(p["q_dec"], state_t)
            state_t = state_t * p["e_last"] + p["kv"]
            sl = slice(c * CHUNK, (c + 1) * CHUNK)
            o_ref[sl, hs] = (_rms(o, gain) * _silu(g_ref[sl, hs])).astype(o_ref.dtype)
        s_ref[g] = state_t


def _hgrn(proj, hgrn_lb, gain, *, bsz, seq, n_heads, col_q, layer):
    group = _pick(n_heads, (4, 3, 2))
    tb = _pick(seq, (256, 128, 64))
    nb = seq // tb
    depth = hgrn_lb.shape[0]
    width = group * HEAD
    assert col_q % group == 0

    def main(col):
        return pl.BlockSpec((tb, width), lambda b, h, i: (b * nb + i, col // group + h))

    return pl.pallas_call(
        functools.partial(_hgrn_kernel, tb=tb, layer=layer, group=group),
        grid=(bsz, n_heads // group, nb),
        in_specs=[main(col_q), main(col_q + n_heads), main(col_q + 2 * n_heads),
                  main(col_q + 3 * n_heads),
                  pl.BlockSpec((depth, width), lambda b, h, i: (0, h)),
                  pl.BlockSpec((1, HEAD), lambda b, h, i: (0, 0))],
        out_specs=pl.BlockSpec((tb, width), lambda b, h, i: (b * nb + i, h)),
        out_shape=jax.ShapeDtypeStruct((bsz * seq, n_heads * HEAD), BF16),
        scratch_shapes=[pltpu.VMEM((group, HEAD, HEAD), F32)],
        compiler_params=_params(3),
        name="hgrn2",
    )(proj, proj, proj, proj, hgrn_lb.astype(F32), gain.reshape(1, HEAD).astype(F32))


def _dil_kernel(q0_ref, q1_ref, q2_ref, kp_ref, kc_ref, vp_ref, vc_ref,
                b0_ref, b1_ref, b2_ref, o_ref, o_scr, l_scr):
    first_super = pl.program_id(2) == 0
    q_refs = (q0_ref, q1_ref, q2_ref)
    b_refs = (b0_ref, b1_ref, b2_ref)
    qi = lax.broadcasted_iota(jnp.int32, (DIL_BLK, DIL_BLK), 0)
    ki = lax.broadcasted_iota(jnp.int32, (DIL_BLK, DIL_BLK), 1)
    valid_prev = ki >= qi
    valid_cur = ki <= qi
    scale = HEAD ** -0.5

    def rows(ref, start, dil):
        if dil == 1:
            return ref[pl.ds(start, DIL_BLK), :]
        return ref[pl.ds(start, DIL_BLK, stride=dil), :]

    for g, (window, dil) in enumerate(DIL_PAIRS):
        span = DIL_BLK * dil
        bias_p = b_refs[g][:, :DIL_BLK]
        bias_c = b_refs[g][:, DIL_BLK:]
        for s in range(DIL_SUPER // span):
            for c in range(dil):
                start = s * span + c
                q = rows(q_refs[g], start, dil)
                k_cur = rows(kc_ref, start, dil)
                v_cur = rows(vc_ref, start, dil)
                if s == 0:
                    k_prev = rows(kp_ref, DIL_SUPER - span + c, dil)
                    v_prev = rows(vp_ref, DIL_SUPER - span + c, dil)
                else:
                    k_prev = rows(kc_ref, start - span, dil)
                    v_prev = rows(vc_ref, start - span, dil)
                lp = jnp.where(valid_prev, _mm_nt(q, k_prev) * scale + bias_p, MASK_VALUE)
                if s == 0:
                    lp = jnp.where(first_super, MASK_VALUE, lp)
                lc = jnp.where(valid_cur, _mm_nt(q, k_cur) * scale + bias_c, MASK_VALUE)
                mx = jnp.maximum(jnp.max(lp, axis=1, keepdims=True),
                                 jnp.max(lc, axis=1, keepdims=True))
                pp = jnp.exp(lp - mx)
                pc = jnp.exp(lc - mx)
                den = jnp.sum(pp, axis=1, keepdims=True) + jnp.sum(pc, axis=1, keepdims=True)
                out = (_mm(pp, v_prev) + _mm(pc, v_cur)) / den
                log_den = jnp.broadcast_to(mx + jnp.log(den), (DIL_BLK, HEAD))
                if dil == 1:
                    o_scr[g, pl.ds(start, DIL_BLK), :] = out
                    l_scr[g, pl.ds(start, DIL_BLK), :] = log_den
                else:
                    o_scr[g, pl.ds(start, DIL_BLK, stride=dil), :] = out
                    l_scr[g, pl.ds(start, DIL_BLK, stride=dil), :] = log_den

    l0, l1, l2 = l_scr[0], l_scr[1], l_scr[2]
    mx = jnp.maximum(jnp.maximum(l0, l1), l2)
    w0, w1, w2 = jnp.exp(l0 - mx), jnp.exp(l1 - mx), jnp.exp(l2 - mx)
    mixed = (w0 * o_scr[0] + w1 * o_scr[1] + w2 * o_scr[2]) / (w0 + w1 + w2)
    o_ref[...] = mixed.astype(o_ref.dtype)


def _t5_bucket(dist):
    max_exact = NUM_BUCKETS // 2
    d = jnp.maximum(dist, 1).astype(F32)
    large = max_exact + (jnp.log(d / max_exact) / math.log(MAX_DISTANCE / max_exact)
                         * (NUM_BUCKETS - max_exact)).astype(jnp.int32)
    large = jnp.clip(large, 0, NUM_BUCKETS - 1)
    return jnp.where(dist < max_exact, dist, large)


def _dil_bias_tables(rel_bias):
    n_kv = rel_bias.shape[1] // len(DIL_PAIRS)
    qi = jnp.arange(DIL_BLK)[:, None]
    ki = jnp.arange(2 * DIL_BLK)[None, :]
    steps = jnp.clip(qi + DIL_BLK - ki, 0, DIL_BLK)
    tables = []
    for g, (_, dil) in enumerate(DIL_PAIRS):
        onehot = (_t5_bucket(steps * dil)[:, :, None] == jnp.arange(NUM_BUCKETS)).astype(F32)
        heads = rel_bias.astype(F32)[:, g * n_kv:(g + 1) * n_kv]
        tables.append(jnp.einsum("qkb,bh->hqk", onehot, heads, precision=lax.Precision.HIGHEST))
    return jnp.concatenate(tables, axis=0)


def _dilated(proj, bias_tbl, *, bsz, seq, n_kv, col_q, col_k, col_v):
    ns = seq // DIL_SUPER

    def qspec(g):
        return pl.BlockSpec((DIL_SUPER, HEAD), lambda b, h, j: (b * ns + j, col_q + g * n_kv + h))

    def cur(col):
        return pl.BlockSpec((DIL_SUPER, HEAD), lambda b, h, j: (b * ns + j, col + h))

    def prev(col):
        return pl.BlockSpec((DIL_SUPER, HEAD),
                            lambda b, h, j: (b * ns + jnp.maximum(j - 1, 0), col + h))

    def bspec(g):
        return pl.BlockSpec((None, DIL_BLK, 2 * DIL_BLK), lambda b, h, j: (g * n_kv + h, 0, 0))

    return pl.pallas_call(
        _dil_kernel,
        grid=(bsz, n_kv, ns),
        in_specs=[qspec(0), qspec(1), qspec(2), prev(col_k), cur(col_k), prev(col_v), cur(col_v),
                  bspec(0), bspec(1), bspec(2)],
        out_specs=pl.BlockSpec((DIL_SUPER, HEAD), lambda b, h, j: (b * ns + j, h)),
        out_shape=jax.ShapeDtypeStruct((bsz * seq, n_kv * HEAD), BF16),
        scratch_shapes=[pltpu.VMEM((len(DIL_PAIRS), DIL_SUPER, HEAD), F32)] * 2,
        compiler_params=_params(3),
        name="dilated_attn",
    )(proj, proj, proj, proj, proj, proj, proj, bias_tbl, bias_tbl, bias_tbl)


def _router_kernel(x_ref, g_ref, w_ref, idx_ref, wgt_ref, *, n_experts):
    hn = _rms(x_ref[...], g_ref[...])
    logits = _mm_split(_split(hn), _split(w_ref[...]))
    lane = lax.broadcasted_iota(jnp.int32, logits.shape, 1)
    logits = jnp.where(lane < n_experts, logits, -jnp.inf)
    m1 = jnp.max(logits, axis=1, keepdims=True)
    i1 = jnp.min(jnp.where(logits == m1, lane, HEAD), axis=1, keepdims=True)
    rest = jnp.where(lane == i1, -jnp.inf, logits)
    m2 = jnp.max(rest, axis=1, keepdims=True)
    i2 = jnp.min(jnp.where(rest == m2, lane, HEAD), axis=1, keepdims=True)
    e2 = jnp.exp(m2 - m1)
    idx_ref[...] = jnp.where(lane == 0, i1, i2)
    wgt_ref[...] = jnp.where(lane == 0, 1.0 / (1.0 + e2), e2 / (1.0 + e2))


def _router(x, gain, w_router):
    t, d = x.shape
    n_experts = w_router.shape[1]
    tm = _pick(t, (256, 128, 64, 8))
    w_pad = jnp.zeros((d, HEAD), F32).at[:, :n_experts].set(w_router.astype(F32))
    return pl.pallas_call(
        functools.partial(_router_kernel, n_experts=n_experts),
        grid=(t // tm,),
        in_specs=[pl.BlockSpec((tm, d), lambda i: (i, 0)),
                  pl.BlockSpec((1, d), lambda i: (0, 0)),
                  pl.BlockSpec((d, HEAD), lambda i: (0, 0))],
        out_specs=[pl.BlockSpec((tm, HEAD), lambda i: (i, 0)),
                   pl.BlockSpec((tm, HEAD), lambda i: (i, 0))],
        out_shape=[jax.ShapeDtypeStruct((t, HEAD), jnp.int32),
                   jax.ShapeDtypeStruct((t, HEAD), F32)],
        compiler_params=_params(1),
        name="router",
    )(x, gain.reshape(1, d).astype(F32), w_pad)


def _routing_plan(top_idx, n_experts, tile):
    t = top_idx.shape[0]
    flat = top_idx.reshape(-1)
    onehot = (flat[:, None] == jnp.arange(n_experts)[None, :]).astype(jnp.int32)
    rank = jnp.sum((jnp.cumsum(onehot, axis=0) - onehot) * onehot, axis=1)
    counts = jnp.sum(onehot, axis=0)
    tiles_per = (counts + tile - 1) // tile
    tile_end = jnp.cumsum(tiles_per)
    dest = (tile_end - tiles_per)[flat] * tile + rank
    n_tiles = (TOP_K * t) // tile + n_experts
    tile_ids = jnp.arange(n_tiles)
    tile_expert = jnp.minimum(jnp.sum((tile_ids[:, None] >= tile_end[None, :]).astype(jnp.int32),
                                      axis=1), n_experts - 1).astype(jnp.int32)
    tile_active = (tile_ids < tile_end[-1]).astype(jnp.int32)
    src = jnp.zeros((n_tiles * tile,), jnp.int32).at[dest].set(
        jnp.arange(TOP_K * t, dtype=jnp.int32) // TOP_K)
    return src, dest.reshape(t, TOP_K).astype(jnp.int32), tile_expert, tile_active


def _row_copy(src_hbm, row, dst_vmem, slot, sem):
    return pltpu.make_async_copy(src_hbm.at[pl.ds(row, 1)], dst_vmem.at[pl.ds(slot, 1)], sem)


def _gather_norm_kernel(src_ref, act_ref, h_hbm, gain_ref, o_ref, buf, sem, *, tile):
    i = pl.program_id(0)
    last = pl.num_programs(0) - 1
    slot = i % 2

    def start_tile(t, s):
        @pl.when(act_ref[t] == 1)
        def _():
            def issue(r, carry):
                _row_copy(h_hbm, src_ref[t * tile + r], buf.at[s], r, sem.at[s]).start()
                return carry

            lax.fori_loop(0, tile, issue, 0, unroll=8)

    @pl.when(i == 0)
    def _():
        start_tile(0, 0)

    @pl.when(i < last)
    def _():
        start_tile(i + 1, 1 - slot)

    @pl.when(act_ref[i] == 1)
    def _():
        def drain(r, carry):
            _row_copy(h_hbm, 0, buf.at[slot], r, sem.at[slot]).wait()
            return carry

        lax.fori_loop(0, tile, drain, 0, unroll=8)
        o_ref[...] = _rms(buf[slot], gain_ref[...]).astype(o_ref.dtype)

    @pl.when(act_ref[i] == 0)
    def _():
        o_ref[...] = jnp.zeros_like(o_ref)


def _gather_norm(h, gain, src, tile_active, *, tile):
    d = h.shape[1]
    n_tiles = tile_active.shape[0]
    return pl.pallas_call(
        functools.partial(_gather_norm_kernel, tile=tile),
        grid_spec=pltpu.PrefetchScalarGridSpec(
            num_scalar_prefetch=2,
            grid=(n_tiles,),
            in_specs=[pl.BlockSpec(memory_space=pl.ANY),
                      pl.BlockSpec((1, d), lambda i, s, a: (0, 0))],
            out_specs=pl.BlockSpec((tile, d), lambda i, s, a: (i, 0)),
            scratch_shapes=[pltpu.VMEM((2, tile, d), F32), pltpu.SemaphoreType.DMA((2,))]),
        out_shape=jax.ShapeDtypeStruct((n_tiles * tile, d), BF16),
        compiler_params=_params(1),
        name="moe_gather",
    )(src, tile_active, h, gain.reshape(1, d).astype(F32))


def _moe_up_kernel(te_ref, act_ref, a_ref, wg_ref, wu_ref, o_ref):
    i = pl.program_id(1)

    @pl.when(act_ref[i] == 1)
    def _():
        a = a_ref[...]
        g = _dot(a, wg_ref[...].astype(BF16))
        u = _dot(a, wu_ref[...].astype(BF16))
        o_ref[...] = (_silu(g) * u).astype(o_ref.dtype)

    @pl.when(act_ref[i] == 0)
    def _():
        o_ref[...] = jnp.zeros_like(o_ref)


def _moe_up(xg, wg, wu, tile_expert, tile_active, *, tile, tn):
    d = xg.shape[1]
    n = wg.shape[-1]
    n_tiles = tile_active.shape[0]
    w_spec = pl.BlockSpec((None, d, tn), lambda j, i, te, a: (te[i], 0, j))
    return pl.pallas_call(
        _moe_up_kernel,
        grid_spec=pltpu.PrefetchScalarGridSpec(
            num_scalar_prefetch=2,
            grid=(n // tn, n_tiles),
            in_specs=[pl.BlockSpec((tile, d), lambda j, i, te, a: (i, 0)), w_spec, w_spec],
            out_specs=pl.BlockSpec((tile, tn), lambda j, i, te, a: (i, j))),
        out_shape=jax.ShapeDtypeStruct((n_tiles * tile, n), BF16),
        compiler_params=_params(2),
        name="moe_up",
    )(tile_expert, tile_active, xg, wg, wu)


def _moe_down_kernel(te_ref, act_ref, a_ref, w_ref, o_ref):
    i = pl.program_id(1)

    @pl.when(act_ref[i] == 1)
    def _():
        o_ref[...] = _dot(a_ref[...], w_ref[...].astype(BF16))

    @pl.when(act_ref[i] == 0)
    def _():
        o_ref[...] = jnp.zeros_like(o_ref)


def _moe_down(act, wd, tile_expert, tile_active, *, tile, tn):
    kd = act.shape[1]
    n = wd.shape[-1]
    n_tiles = tile_active.shape[0]
    return pl.pallas_call(
        _moe_down_kernel,
        grid_spec=pltpu.PrefetchScalarGridSpec(
            num_scalar_prefetch=2,
            grid=(n // tn, n_tiles),
            in_specs=[pl.BlockSpec((tile, kd), lambda j, i, te, a: (i, 0)),
                      pl.BlockSpec((None, kd, tn), lambda j, i, te, a: (te[i], 0, j))],
            out_specs=pl.BlockSpec((tile, tn), lambda j, i, te, a: (i, j))),
        out_shape=jax.ShapeDtypeStruct((n_tiles * tile, n), F32),
        compiler_params=_params(2),
        name="moe_down",
    )(tile_expert, tile_active, act, wd)


def _combine_kernel(d0_ref, d1_ref, y_hbm, h_ref, w_ref, g_ref, o_ref, buf0, buf1, sem, *, tc,
                    final_norm):
    i = pl.program_id(0)
    slot = i % 2

    def start_tile(t, s):
        def issue(r, carry):
            _row_copy(y_hbm, d0_ref[t * tc + r], buf0.at[s], r, sem.at[s]).start()
            _row_copy(y_hbm, d1_ref[t * tc + r], buf1.at[s], r, sem.at[s]).start()
            return carry

        lax.fori_loop(0, tc, issue, 0, unroll=8)

    @pl.when(i == 0)
    def _():
        start_tile(0, 0)

    @pl.when(i < pl.num_programs(0) - 1)
    def _():
        start_tile(i + 1, 1 - slot)

    def drain(r, carry):
        _row_copy(y_hbm, 0, buf0.at[slot], r, sem.at[slot]).wait()
        _row_copy(y_hbm, 0, buf1.at[slot], r, sem.at[slot]).wait()
        return carry

    lax.fori_loop(0, tc, drain, 0, unroll=8)
    w = w_ref[...]
    out = h_ref[...] + (w[:, 0:1] * buf0[slot] + w[:, 1:2] * buf1[slot])
    if final_norm:
        out = _rms(out, g_ref[...])
    o_ref[...] = out


def _combine(h, yg, dest, weights, final_gain):
    t, d = h.shape
    tc = _pick(t, (256, 128, 64, 8))
    final_norm = final_gain is not None
    gain = (final_gain if final_norm else jnp.ones((d,), F32)).reshape(1, d).astype(F32)
    return pl.pallas_call(
        functools.partial(_combine_kernel, tc=tc, final_norm=final_norm),
        grid_spec=pltpu.PrefetchScalarGridSpec(
            num_scalar_prefetch=2,
            grid=(t // tc,),
            in_specs=[pl.BlockSpec(memory_space=pl.ANY),
                      pl.BlockSpec((tc, d), lambda i, a, b: (i, 0)),
                      pl.BlockSpec((tc, HEAD), lambda i, a, b: (i, 0)),
                      pl.BlockSpec((1, d), lambda i, a, b: (0, 0))],
            out_specs=pl.BlockSpec((tc, d), lambda i, a, b: (i, 0)),
            scratch_shapes=[pltpu.VMEM((2, tc, d), F32), pltpu.VMEM((2, tc, d), F32),
                            pltpu.SemaphoreType.DMA((2,))]),
        out_shape=jax.ShapeDtypeStruct((t, d), F32),
        compiler_params=_params(1),
        name="moe_combine",
    )(dest[:, 0], dest[:, 1], yg, h, weights, gain)


def _moe_tile(rows):
    return 768 if rows >= 6144 else _pick(rows, (256, 128))


def _moe(h, gain, w_router, wg, wu, wd, final_gain):
    t, d = h.shape
    n_experts = w_router.shape[1]
    d_ff = wg.shape[-1]
    tile = _moe_tile(TOP_K * t)
    top_idx, top_w = _router(h, gain, w_router)
    src, dest, tile_expert, tile_active = _routing_plan(top_idx[:, :TOP_K], n_experts, tile)
    xg = _gather_norm(h, gain, src, tile_active, tile=tile)
    act = _moe_up(xg, wg, wu, tile_expert, tile_active, tile=tile, tn=_pick(d_ff, (512, 256, 128)))
    yg = _moe_down(act, wd, tile_expert, tile_active, tile=tile, tn=_pick(d, (512, 256, 128)))
    return _combine(h, yg, dest, top_w, final_gain)


def kernel(x, w_in, conv_gdn, gdn_a_log, gdn_dt_bias, gdn_norm, hgrn_lb, hgrn_norm, rel_bias,
           w_out, norm_mix, norm_ffn, w_gate_dense, w_up_dense, w_down_dense, w_router,
           w_gate_moe, w_up_moe, w_down_moe, norm_final):
    bsz, seq, d = x.shape
    depth = w_in.shape[0]
    t = bsz * seq
    n_gdn = gdn_a_log.shape[1]
    n_hgrn = hgrn_lb.shape[1] // HEAD
    n_q = rel_bias.shape[1]
    n_kv = n_q // len(DIL_PAIRS)

    split = 4 * n_gdn * HEAD
    shift = 2 * n_gdn
    rest = w_in.shape[-1] - split - shift
    tn_a = _pick(split, (512, 256, 128))
    tn_b = _pick(math.gcd(rest, split), (512, 256, 128))
    col_hq = 0
    col_dq = 4 * n_hgrn
    col_dk = col_dq + n_q
    col_dv = col_dk + n_kv

    w_in_t = jnp.swapaxes(w_in, 1, 2)
    bias_tbl = _dil_bias_tables(rel_bias)
    tm = _pick(t, (1024, 512, 256, 128))
    final_done = False

    h = x.reshape(t, d).astype(F32)
    for layer in range(depth):
        hn = _rmsnorm(h, norm_mix[layer], BF16)
        proj_a = _proj_t(hn, w_in_t, layer, row0=0, shift=0, width=split + tn_a, tm=tm, tn=tn_a)
        proj_b = _proj_t(hn, w_in_t, layer, row0=split, shift=shift, width=rest, tm=tm, tn=tn_b)
        oa = _gdn(proj_a, conv_gdn[layer].astype(F32), gdn_a_log[layer], gdn_dt_bias[layer],
                  gdn_norm[layer], bsz=bsz, seq=seq, n_heads=n_gdn, col_q=0, col_z=3 * n_gdn,
                  col_tail=split // HEAD)
        ob = _hgrn(proj_b, hgrn_lb, hgrn_norm[layer], bsz=bsz, seq=seq, n_heads=n_hgrn,
                   col_q=col_hq, layer=layer)
        oc = _dilated(proj_b, bias_tbl, bsz=bsz, seq=seq, n_kv=n_kv, col_q=col_dq, col_k=col_dk,
                      col_v=col_dv)
        h = _out_proj([oa, ob, oc], w_out[layer], h, tm=tm, tn=_pick(d, (512, 256, 128)))

        idx = layer // 2
        if layer % 2 == 0:
            hn = _rmsnorm(h, norm_ffn[layer], BF16)
            d_ff = w_gate_dense.shape[-1]
            act = _swiglu_up(hn, w_gate_dense[idx], w_up_dense[idx], tm=tm,
                             tn=_pick(d_ff, (256, 128)))
            h = _matmul(act, w_down_dense[idx], tm=tm, tn=_pick(d, (1024, 512, 256, 128)),
                        tk=_pick(d_ff, (2048, 1792, 1024, 512, 256, 128)), residual=h,
                        name="ffn_down")
        else:
            final_done = layer == depth - 1
            h = _moe(h, norm_ffn[layer], w_router[idx], w_gate_moe[idx], w_up_moe[idx],
                     w_down_moe[idx], norm_final if final_done else None)

    out = h if final_done else _rmsnorm(h, norm_final, F32)
    return out.reshape(bsz, seq, d).astype(x.dtype)
```

```python
import functools
import math

import jax
import jax.numpy as jnp
from jax import lax
from jax.experimental import pallas as pl
from jax.experimental.pallas import tpu as pltpu

F32 = jnp.float32
BF16 = jnp.bfloat16

HEAD = 128
CHUNK = 64
SUB = 8
CONV_WIDTH = 4
DIL_PAIRS = ((128, 1), (512, 4), (2048, 16))
DIL_BLK = 128
DIL_SUPER = 2048
DIL_UNITS = 4
NUM_BUCKETS = 32
MAX_DISTANCE = 2048
TOP_K = 2
RMS_EPS = 1e-6
MASK_VALUE = -1e30
MIN_GATE = 1e-20
VMEM_LIMIT = 56 * 1024 * 1024


def _params(n_axes):
    return pltpu.CompilerParams(dimension_semantics=("arbitrary",) * n_axes,
                                vmem_limit_bytes=VMEM_LIMIT)


def _pick(n, cands):
    for c in cands:
        if n % c == 0:
            return c
    return n


def _dot(a, b):
    return jnp.dot(a, b, preferred_element_type=F32)


def _mm(a, b):
    return _dot(a.astype(BF16), b.astype(BF16))


def _mm_nt(a, b):
    return lax.dot_general(a.astype(BF16), b.astype(BF16), (((1,), (1,)), ((), ())),
                           preferred_element_type=F32)


def _mm_tn(a, b):
    return lax.dot_general(a.astype(BF16), b.astype(BF16), (((0,), (0,)), ((), ())),
                           preferred_element_type=F32)


def _split(x):
    hi = x.astype(BF16)
    return hi, (x - hi.astype(F32)).astype(BF16)


def _mm_split(a, b):
    return _dot(a[0], b[0]) + (_dot(a[0], b[1]) + _dot(a[1], b[0]))


def _sigmoid(x):
    return 1.0 / (1.0 + jnp.exp(-x))


def _silu(x):
    return x * _sigmoid(x)


def _softplus(x):
    return jnp.maximum(x, 0.0) + jnp.log(1.0 + jnp.exp(-jnp.abs(x)))


def _rms(x, gain):
    return x * lax.rsqrt(jnp.mean(x * x, axis=-1, keepdims=True) + RMS_EPS) * gain


def _chunk_cumsum(x, rin):
    s = 1
    while s < CHUNK:
        x = x + jnp.where(rin >= s, pltpu.roll(x, s, 0), 0.0)
        s *= 2
    return x


def _rmsnorm_kernel(x_ref, g_ref, o_ref):
    o_ref[...] = _rms(x_ref[...], g_ref[...]).astype(o_ref.dtype)


def _rmsnorm(x, gain, out_dtype):
    t, d = x.shape
    tm = _pick(t, (256, 128, 64, 8))
    return pl.pallas_call(
        _rmsnorm_kernel,
        grid=(t // tm,),
        in_specs=[pl.BlockSpec((tm, d), lambda i: (i, 0)),
                  pl.BlockSpec((1, d), lambda i: (0, 0))],
        out_specs=pl.BlockSpec((tm, d), lambda i: (i, 0)),
        out_shape=jax.ShapeDtypeStruct((t, d), out_dtype),
        compiler_params=_params(1),
        name="rmsnorm",
    )(x, gain.reshape(1, d).astype(F32))


def _matmul_kernel(*refs, nk, has_res):
    a_ref, b_ref = refs[0], refs[1]
    r_ref = refs[2] if has_res else None
    o_ref = refs[-1]
    part = _dot(a_ref[...], b_ref[...].astype(BF16))

    def first():
        return part if r_ref is None else r_ref[...] + part

    if nk == 1:
        o_ref[...] = first()
        return
    @pl.when(pl.program_id(2) == 0)
    def _():
        o_ref[...] = r_ref[...] if r_ref is not None else jnp.zeros_like(o_ref)

    o_ref[...] += part


def _matmul(a, b, *, tm, tn, tk, residual=None, name="matmul"):
    m, kd = a.shape
    n = b.shape[-1]
    nk = kd // tk
    in_specs = [pl.BlockSpec((tm, tk), lambda i, j, k: (i, k)),
                pl.BlockSpec((tk, tn), lambda i, j, k: (k, j))]
    args = [a, b]
    if residual is not None:
        in_specs.append(pl.BlockSpec((tm, tn), lambda i, j, k: (i, j),
                                     pipeline_mode=pl.Buffered(1)))
        args.append(residual)
    return pl.pallas_call(
        functools.partial(_matmul_kernel, nk=nk, has_res=residual is not None),
        grid=(m // tm, n // tn, nk),
        in_specs=in_specs,
        out_specs=pl.BlockSpec((tm, tn), lambda i, j, k: (i, j)),
        out_shape=jax.ShapeDtypeStruct((m, n), F32),
        compiler_params=_params(3),
        name=name,
    )(*args)


def _proj_t_kernel(*refs, shift, chunk):
    a_ref, wm_ref = refs[0], refs[1]
    wx_ref = refs[2] if shift else None
    o_ref, w_scr = refs[-2], refs[-1]
    kd, tn = w_scr.shape

    @pl.when(pl.program_id(1) == 0)
    def _():
        for r in range(kd // chunk):
            cs = slice(r * chunk, (r + 1) * chunk)
            blk = wm_ref[:, cs]
            if shift:
                cat = jnp.concatenate([blk, wx_ref[:, cs]], axis=0)
                blk = pltpu.roll(cat, tn + HEAD - shift, 0)[:tn]
            w_scr[cs, :] = blk.T.astype(BF16)

    o_ref[...] = _dot(a_ref[...], w_scr[...])


def _proj_t(a, wt, lead, *, row0, shift, width, tm, tn):
    m, kd = a.shape
    in_specs = [pl.BlockSpec((tm, kd), lambda j, i: (i, 0)),
                pl.BlockSpec((None, tn, kd), lambda j, i: (lead, row0 // tn + j, 0))]
    args = [a, wt]
    if shift:
        in_specs.append(pl.BlockSpec((None, HEAD, kd),
                                     lambda j, i: (lead, (row0 + tn * (j + 1)) // HEAD, 0)))
        args.append(wt)
    return pl.pallas_call(
        functools.partial(_proj_t_kernel, shift=shift, chunk=_pick(kd, (512, 256, 128))),
        grid=(width // tn, m // tm),
        in_specs=in_specs,
        out_specs=pl.BlockSpec((tm, tn), lambda j, i: (i, j)),
        out_shape=jax.ShapeDtypeStruct((m, width), F32),
        scratch_shapes=[pltpu.VMEM((kd, tn), BF16)],
        compiler_params=_params(2),
        name="in_proj",
    )(*args)


def _swiglu_up_kernel(a_ref, wg_ref, wu_ref, o_ref):
    a = a_ref[...]
    g = _dot(a, wg_ref[...].astype(BF16))
    u = _dot(a, wu_ref[...].astype(BF16))
    o_ref[...] = (_silu(g) * u).astype(o_ref.dtype)


def _swiglu_up(a, wg, wu, *, tm, tn):
    m, kd = a.shape
    n = wg.shape[-1]
    w_spec = pl.BlockSpec((kd, tn), lambda i, j: (0, j))
    return pl.pallas_call(
        _swiglu_up_kernel,
        grid=(m // tm, n // tn),
        in_specs=[pl.BlockSpec((tm, kd), lambda i, j: (i, 0)), w_spec, w_spec],
        out_specs=pl.BlockSpec((tm, tn), lambda i, j: (i, j)),
        out_shape=jax.ShapeDtypeStruct((m, n), BF16),
        compiler_params=_params(2),
        name="swiglu_up",
    )(a, wg, wu)


def _out_proj_kernel(*refs, n_pieces):
    a_refs = refs[:n_pieces]
    w_refs = refs[n_pieces:2 * n_pieces]
    r_ref, o_ref = refs[2 * n_pieces], refs[2 * n_pieces + 1]
    acc = r_ref[...]
    for a_ref, w_ref in zip(a_refs, w_refs):
        acc = acc + _dot(a_ref[...], w_ref[...].astype(BF16))
    o_ref[...] = acc


def _out_proj(parts, w, residual, *, tm, tn):
    m = residual.shape[0]
    n = w.shape[1]
    unit = HEAD * functools.reduce(math.gcd, [p.shape[1] // HEAD for p in parts])
    a_specs, a_args = [], []
    for p in parts:
        for c in range(p.shape[1] // unit):
            a_specs.append(pl.BlockSpec((tm, unit), lambda i, j, c=c: (i, c)))
            a_args.append(p)
    n_pieces = len(a_args)
    w_specs = [pl.BlockSpec((unit, tn), lambda i, j, c=c: (c, j)) for c in range(n_pieces)]
    return pl.pallas_call(
        functools.partial(_out_proj_kernel, n_pieces=n_pieces),
        grid=(m // tm, n // tn),
        in_specs=a_specs + w_specs + [pl.BlockSpec((tm, tn), lambda i, j: (i, j))],
        out_specs=pl.BlockSpec((tm, tn), lambda i, j: (i, j)),
        out_shape=jax.ShapeDtypeStruct((m, n), F32),
        compiler_params=_params(2),
        name="out_proj",
    )(*a_args, *([w] * n_pieces), residual)


def _gdn_kernel(alog_ref, dtb_ref, q_ref, qh_ref, k_ref, kh_ref, v_ref, vh_ref, z_ref, t_ref,
                wq_ref, wk_ref, wv_ref, gain_ref, o_ref, s_ref, *, tb, n_heads, group):
    hg = pl.program_id(1)
    first = pl.program_id(2) == 0

    @pl.when(first)
    def _():
        s_ref[...] = jnp.zeros_like(s_ref)

    def conv_silu(x_ref, xh_ref, w_ref):
        halo = jnp.where(first, 0.0, xh_ref[...])
        ext = jnp.concatenate([halo, x_ref[...]], axis=0)
        w = w_ref[...]
        y = pltpu.roll(ext, 3, 0)[SUB:] * w[0:1]
        y = y + pltpu.roll(ext, 2, 0)[SUB:] * w[1:2]
        y = y + pltpu.roll(ext, 1, 0)[SUB:] * w[2:3]
        y = y + ext[SUB:] * w[3:4]
        return _silu(y)

    def l2norm(x):
        return x * lax.rsqrt(jnp.sum(x * x, axis=-1, keepdims=True) + RMS_EPS)

    q_all = conv_silu(q_ref, qh_ref, wq_ref)
    k_all = conv_silu(k_ref, kh_ref, wk_ref)
    v_all = conv_silu(v_ref, vh_ref, wv_ref)
    tail = t_ref[...]
    lane = lax.broadcasted_iota(jnp.int32, (tb, HEAD), 1)
    rin = lax.broadcasted_iota(jnp.int32, (tb, HEAD), 0) & (CHUNK - 1)

    row = lax.broadcasted_iota(jnp.int32, (CHUNK, CHUNK), 0)
    col = lax.broadcasted_iota(jnp.int32, (CHUNK, CHUNK), 1)
    eye = (row == col).astype(F32)
    causal = row >= col
    strict = row > col
    level_masks = [(((row >> k) == (col >> k)) & ((row >> (k - 1)) != (col >> (k - 1)))).astype(F32)
                   for k in range(1, 7)]

    n_chunks = tb // CHUNK
    units = []
    for g in range(group):
        h = hg * group + g
        hs = slice(g * HEAD, (g + 1) * HEAD)
        q_h = l2norm(q_all[:, hs]) * (HEAD ** -0.5)
        k_h = l2norm(k_all[:, hs])
        v_h = v_all[:, hs]
        b_col = jnp.sum(jnp.where(lane == h, tail, 0.0), axis=1, keepdims=True)
        a_col = jnp.sum(jnp.where(lane == h + n_heads, tail, 0.0), axis=1, keepdims=True)
        decay_rate = jnp.exp(jnp.zeros((1, 1), F32) + alog_ref[h])
        log_decay = -decay_rate * _softplus(a_col + dtb_ref[h])
        beta_h = jnp.broadcast_to(_sigmoid(b_col), (tb, HEAD))
        gc_h = _chunk_cumsum(jnp.broadcast_to(log_decay, (tb, HEAD)), rin)
        for c in range(n_chunks):
            sl = slice(c * CHUNK, (c + 1) * CHUNK)
            units.append(dict(q=q_h[sl], k=k_h[sl], v=v_h[sl], beta=beta_h[sl], gc=gc_h[sl]))

    for p in units:
        gcol = p["gc"][:, :CHUNK]
        grow = jnp.sum(gcol * eye, axis=0, keepdims=True)
        p["decay"] = jnp.where(causal, jnp.exp(jnp.where(causal, gcol - grow, 0.0)), 0.0)
        p["egc"] = jnp.exp(p["gc"])
        p["kb"] = p["k"] * p["beta"]
    for p in units:
        p["lower"] = jnp.where(strict, _mm_nt(p["kb"], p["k"]) * p["decay"], 0.0)
        p["attn"] = _mm_nt(p["q"], p["k"]) * p["decay"]
        p["inv"] = eye - p["lower"] * level_masks[0]
    for lvl in range(1, 6):
        for p in units:
            p["inv_s"] = _split(p["inv"])
            p["step"] = _mm_split(p["inv_s"], _split(p["lower"] * level_masks[lvl]))
        for p in units:
            p["inv"] = p["inv"] - _mm_split(_split(p["step"]), p["inv_s"])
    for p in units:
        inv_s = _split(p["inv"])
        p["u"] = _mm_split(inv_s, _split(p["v"] * p["beta"]))
        p["w"] = _mm_split(inv_s, _split(p["kb"] * p["egc"]))
        g_last = p["gc"][CHUNK - 1:CHUNK, :]
        p["q_dec"] = p["q"] * p["egc"]
        p["k_dec"] = p["k"] * jnp.exp(g_last - p["gc"])
        p["e_last"] = jnp.exp(g_last)

    gain = gain_ref[...]
    states = [s_ref[g] for g in range(group)]
    for c in range(n_chunks):
        ps = [units[g * n_chunks + c] for g in range(group)]
        v_new = [p["u"] - _mm(p["w"], s) for p, s in zip(ps, states)]
        outs = [_mm(p["q_dec"], s) + _mm(p["attn"], vn) for p, s, vn in zip(ps, states, v_new)]
        states = [s * p["e_last"] + _mm_tn(p["k_dec"], vn) for p, s, vn in zip(ps, states, v_new)]
        sl = slice(c * CHUNK, (c + 1) * CHUNK)
        for g, o in enumerate(outs):
            hs = slice(g * HEAD, (g + 1) * HEAD)
            o_ref[sl, hs] = (_rms(o, gain) * _silu(z_ref[sl, hs])).astype(o_ref.dtype)
    for g in range(group):
        s_ref[g] = states[g]


def _gdn(proj, conv_w, a_log, dt_bias, gain, *, bsz, seq, n_heads, col_q, col_z, col_tail):
    group = _pick(n_heads, (4, 3, 2))
    tb = _pick(seq, (256, 128, 64))
    nb = seq // tb
    width = group * HEAD

    def main(col):
        return pl.BlockSpec((tb, width), lambda b, h, i: (b * nb + i, col // group + h))

    def halo(col):
        return pl.BlockSpec(
            (SUB, width),
            lambda b, h, i: (jnp.maximum(b * (seq // SUB) + i * (tb // SUB) - 1, 0),
                             col // group + h))

    def wspec(col):
        return pl.BlockSpec((CONV_WIDTH, width), lambda b, h, i: (0, col // group + h))

    smem = pl.BlockSpec(memory_space=pltpu.SMEM)
    col_k = col_q + n_heads
    col_v = col_q + 2 * n_heads
    assert all(c % group == 0 for c in (col_q, col_k, col_v, col_z))
    return pl.pallas_call(
        functools.partial(_gdn_kernel, tb=tb, n_heads=n_heads, group=group),
        grid=(bsz, n_heads // group, nb),
        in_specs=[smem, smem,
                  main(col_q), halo(col_q), main(col_k), halo(col_k), main(col_v), halo(col_v),
                  main(col_z),
                  pl.BlockSpec((tb, HEAD), lambda b, h, i: (b * nb + i, col_tail)),
                  wspec(0), wspec(n_heads), wspec(2 * n_heads),
                  pl.BlockSpec((1, HEAD), lambda b, h, i: (0, 0))],
        out_specs=pl.BlockSpec((tb, width), lambda b, h, i: (b * nb + i, h)),
        out_shape=jax.ShapeDtypeStruct((bsz * seq, n_heads * HEAD), BF16),
        scratch_shapes=[pltpu.VMEM((group, HEAD, HEAD), F32)],
        compiler_params=_params(3),
        name="gdn",
    )(a_log.astype(F32), dt_bias.astype(F32), proj, proj, proj, proj, proj, proj, proj, proj,
      conv_w, conv_w, conv_w, gain.reshape(1, HEAD).astype(F32))


def _hgrn_kernel(q_ref, f_ref, i_ref, g_ref, lb_ref, gain_ref, o_ref, s_ref, *, tb, layer, group):
    @pl.when(pl.program_id(2) == 0)
    def _():
        s_ref[...] = jnp.zeros_like(s_ref)

    lbp = lb_ref[...]
    e = jnp.exp(lbp - jnp.max(lbp, axis=0, keepdims=True))
    probs = e / jnp.sum(e, axis=0, keepdims=True)
    lower = jnp.sum(probs[:layer + 1], axis=0, keepdims=True) - probs[0:1]

    f_pre = f_ref[...]
    f_gate = lower + (1.0 - lower) * _sigmoid(f_pre)
    log_f = jnp.log(jnp.maximum(f_gate, MIN_GATE))
    rin = lax.broadcasted_iota(jnp.int32, f_pre.shape, 0) & (CHUNK - 1)
    b_all = _chunk_cumsum(log_f, rin)
    k_all = (1.0 - lower) * _sigmoid(-f_pre)
    q_all = _silu(q_ref[...])
    v_all = i_ref[...]

    row = lax.broadcasted_iota(jnp.int32, (CHUNK, CHUNK), 0)
    col = lax.broadcasted_iota(jnp.int32, (CHUNK, CHUNK), 1)
    sub_row = lax.broadcasted_iota(jnp.int32, (SUB, 1), 0)
    off_masks = {1 << sh: ((((row >> sh) & 1) == 1) & ((col >> sh) == (row >> sh) - 1)).astype(F32)
                 for sh in (3, 4, 5)}
    gain = gain_ref[...]
    n_sub = CHUNK // SUB
    n_chunks = tb // CHUNK

    units = []
    for g in range(group):
        hs = slice(g * HEAD, (g + 1) * HEAD)
        for c in range(n_chunks):
            sl = slice(c * CHUNK, (c + 1) * CHUNK)
            units.append(dict(q=q_all[sl, hs], k=k_all[sl, hs], b=b_all[sl, hs], v=v_all[sl, hs]))

    for p in units:
        q, k, b = p["q"], p["k"], p["b"]
        b_last = b[CHUNK - 1:CHUNK, :]
        p["q_dec"] = q * jnp.exp(b)
        p["e_last"] = jnp.exp(b_last)
        p["kv"] = _mm_tn(p["v"], k * jnp.exp(b_last - b))
    for m in (8, 16, 32):
        for p in units:
            q, k, b = p["q"], p["k"], p["b"]
            ref_q, ref_k = [], []
            for blk in range(n_sub):
                start = (blk * SUB // m) * m
                ref_q.append(jnp.broadcast_to(b[start:start + 1, :], (SUB, HEAD)))
                nxt = start + m
                if nxt < CHUNK:
                    ref_k.append(jnp.broadcast_to(b[nxt:nxt + 1, :], (SUB, HEAD)))
                else:
                    ref_k.append(b[blk * SUB:(blk + 1) * SUB, :])
            qe = q * jnp.exp(b - jnp.concatenate(ref_q, axis=0))
            ke = k * jnp.exp(jnp.concatenate(ref_k, axis=0) - b)
            part = _mm_nt(qe, ke) * off_masks[m]
            p["scores"] = part if m == 8 else p["scores"] + part
    for p in units:
        p["intra"] = _mm(p["scores"], p["v"])
    for p in units:
        q, k, b, v = p["q"], p["k"], p["b"], p["v"]
        diag = []
        for blk in range(n_sub):
            sl = slice(blk * SUB, (blk + 1) * SUB)
            qi, ki, bi, vi = q[sl], k[sl], b[sl], v[sl]
            acc = jnp.zeros((SUB, HEAD), F32)
            for s in range(SUB):
                m = sub_row >= s
                dec = jnp.where(m, jnp.exp(jnp.where(m, bi - bi[s:s + 1], 0.0)), 0.0)
                wgt = jnp.sum(qi * ki[s:s + 1] * dec, axis=1, keepdims=True)
                acc = acc + wgt * vi[s:s + 1]
            diag.append(acc)
        p["intra"] = p["intra"] + jnp.concatenate(diag, axis=0)

    for g in range(group):
        hs = slice(g * HEAD, (g + 1) * HEAD)
        state_t = s_ref[g]
        for c in range(n_chunks):
            p = units[g * n_chunks + c]
            o = p["intra"] + _mm_nt(p["q_dec"], state_t)
            state_t = state_t * p["e_last"] + p["kv"]
            sl = slice(c * CHUNK, (c + 1) * CHUNK)
            o_ref[sl, hs] = (_rms(o, gain) * _silu(g_ref[sl, hs])).astype(o_ref.dtype)
        s_ref[g] = state_t


def _hgrn(proj, hgrn_lb, gain, *, bsz, seq, n_heads, col_q, layer):
    group = _pick(n_heads, (4, 3, 2))
    tb = _pick(seq, (256, 128, 64))
    nb = seq // tb
    depth = hgrn_lb.shape[0]
    width = group * HEAD
    assert col_q % group == 0

    def main(col):
        return pl.BlockSpec((tb, width), lambda b, h, i: (b * nb + i, col // group + h))

    return pl.pallas_call(
        functools.partial(_hgrn_kernel, tb=tb, layer=layer, group=group),
        grid=(bsz, n_heads // group, nb),
        in_specs=[main(col_q), main(col_q + n_heads), main(col_q + 2 * n_heads),
                  main(col_q + 3 * n_heads),
                  pl.BlockSpec((depth, width), lambda b, h, i: (0, h)),
                  pl.BlockSpec((1, HEAD), lambda b, h, i: (0, 0))],
        out_specs=pl.BlockSpec((tb, width), lambda b, h, i: (b * nb + i, h)),
        out_shape=jax.ShapeDtypeStruct((bsz * seq, n_heads * HEAD), BF16),
        scratch_shapes=[pltpu.VMEM((group, HEAD, HEAD), F32)],
        compiler_params=_params(3),
        name="hgrn2",
    )(proj, proj, proj, proj, hgrn_lb.astype(F32), gain.reshape(1, HEAD).astype(F32))


def _dil_kernel(q0_ref, q1_ref, q2_ref, kp_ref, kc_ref, vp_ref, vc_ref,
                b0_ref, b1_ref, b2_ref, o_ref, o_scr, l_scr):
    first_super = pl.program_id(2) == 0
    q_refs = (q0_ref, q1_ref, q2_ref)
    b_refs = (b0_ref, b1_ref, b2_ref)
    qi = lax.broadcasted_iota(jnp.int32, (DIL_BLK, DIL_BLK), 0)
    ki = lax.broadcasted_iota(jnp.int32, (DIL_BLK, DIL_BLK), 1)
    valid_prev = ki >= qi
    valid_cur = ki <= qi
    scale = HEAD ** -0.5

    def rows(ref, start, dil):
        if dil == 1:
            return ref[pl.ds(start, DIL_BLK), :]
        return ref[pl.ds(start, DIL_BLK, stride=dil), :]

    for g, (window, dil) in enumerate(DIL_PAIRS):
        span = DIL_BLK * dil
        bias_p = b_refs[g][:, :DIL_BLK]
        bias_c = b_refs[g][:, DIL_BLK:]
        starts = [(s, s * span + c) for s in range(DIL_SUPER // span) for c in range(dil)]
        for u0 in range(0, len(starts), DIL_UNITS):
            units = []
            for s, start in starts[u0:u0 + DIL_UNITS]:
                before = (kp_ref, vp_ref, DIL_SUPER - span + start) if s == 0 else \
                    (kc_ref, vc_ref, start - span)
                units.append(dict(
                    s=s, start=start, q=rows(q_refs[g], start, dil),
                    k_cur=rows(kc_ref, start, dil), v_cur=rows(vc_ref, start, dil),
                    k_prev=rows(before[0], before[2], dil), v_prev=rows(before[1], before[2], dil)))
            for p in units:
                lp = jnp.where(valid_prev, _mm_nt(p["q"], p["k_prev"]) * scale + bias_p, MASK_VALUE)
                if p["s"] == 0:
                    lp = jnp.where(first_super, MASK_VALUE, lp)
                p["lp"] = lp
                p["lc"] = jnp.where(valid_cur, _mm_nt(p["q"], p["k_cur"]) * scale + bias_c,
                                    MASK_VALUE)
            for p in units:
                p["mx"] = jnp.maximum(jnp.max(p["lp"], axis=1, keepdims=True),
                                      jnp.max(p["lc"], axis=1, keepdims=True))
            for p in units:
                p["pp"] = jnp.exp(p["lp"] - p["mx"])
                p["pc"] = jnp.exp(p["lc"] - p["mx"])
                p["den"] = (jnp.sum(p["pp"], axis=1, keepdims=True)
                            + jnp.sum(p["pc"], axis=1, keepdims=True))
            for p in units:
                out = (_mm(p["pp"], p["v_prev"]) + _mm(p["pc"], p["v_cur"])) / p["den"]
                log_den = jnp.broadcast_to(p["mx"] + jnp.log(p["den"]), (DIL_BLK, HEAD))
                if dil == 1:
                    dst = pl.ds(p["start"], DIL_BLK)
                else:
                    dst = pl.ds(p["start"], DIL_BLK, stride=dil)
                o_scr[g, dst, :] = out
                l_scr[g, dst, :] = log_den

    l0, l1, l2 = l_scr[0], l_scr[1], l_scr[2]
    mx = jnp.maximum(jnp.maximum(l0, l1), l2)
    w0, w1, w2 = jnp.exp(l0 - mx), jnp.exp(l1 - mx), jnp.exp(l2 - mx)
    mixed = (w0 * o_scr[0] + w1 * o_scr[1] + w2 * o_scr[2]) / (w0 + w1 + w2)
    o_ref[...] = mixed.astype(o_ref.dtype)


def _t5_bucket(dist):
    max_exact = NUM_BUCKETS // 2
    d = jnp.maximum(dist, 1).astype(F32)
    large = max_exact + (jnp.log(d / max_exact) / math.log(MAX_DISTANCE / max_exact)
                         * (NUM_BUCKETS - max_exact)).astype(jnp.int32)
    large = jnp.clip(large, 0, NUM_BUCKETS - 1)
    return jnp.where(dist < max_exact, dist, large)


def _dil_bias_tables(rel_bias):
    n_kv = rel_bias.shape[1] // len(DIL_PAIRS)
    qi = jnp.arange(DIL_BLK)[:, None]
    ki = jnp.arange(2 * DIL_BLK)[None, :]
    steps = jnp.clip(qi + DIL_BLK - ki, 0, DIL_BLK)
    tables = []
    for g, (_, dil) in enumerate(DIL_PAIRS):
        onehot = (_t5_bucket(steps * dil)[:, :, None] == jnp.arange(NUM_BUCKETS)).astype(F32)
        heads = rel_bias.astype(F32)[:, g * n_kv:(g + 1) * n_kv]
        tables.append(jnp.einsum("qkb,bh->hqk", onehot, heads, precision=lax.Precision.HIGHEST))
    return jnp.concatenate(tables, axis=0)


def _dilated(proj, bias_tbl, *, bsz, seq, n_kv, col_q, col_k, col_v):
    ns = seq // DIL_SUPER

    def qspec(g):
        return pl.BlockSpec((DIL_SUPER, HEAD), lambda b, h, j: (b * ns + j, col_q + g * n_kv + h))

    def cur(col):
        return pl.BlockSpec((DIL_SUPER, HEAD), lambda b, h, j: (b * ns + j, col + h))

    def prev(col):
        return pl.BlockSpec((DIL_SUPER, HEAD),
                            lambda b, h, j: (b * ns + jnp.maximum(j - 1, 0), col + h))

    def bspec(g):
        return pl.BlockSpec((None, DIL_BLK, 2 * DIL_BLK), lambda b, h, j: (g * n_kv + h, 0, 0))

    return pl.pallas_call(
        _dil_kernel,
        grid=(bsz, n_kv, ns),
        in_specs=[qspec(0), qspec(1), qspec(2), prev(col_k), cur(col_k), prev(col_v), cur(col_v),
                  bspec(0), bspec(1), bspec(2)],
        out_specs=pl.BlockSpec((DIL_SUPER, HEAD), lambda b, h, j: (b * ns + j, h)),
        out_shape=jax.ShapeDtypeStruct((bsz * seq, n_kv * HEAD), BF16),
        scratch_shapes=[pltpu.VMEM((len(DIL_PAIRS), DIL_SUPER, HEAD), F32)] * 2,
        compiler_params=_params(3),
        name="dilated_attn",
    )(proj, proj, proj, proj, proj, proj, proj, bias_tbl, bias_tbl, bias_tbl)


def _router_kernel(x_ref, g_ref, w_ref, idx_ref, wgt_ref, *, n_experts):
    hn = _rms(x_ref[...], g_ref[...])
    logits = _mm_split(_split(hn), _split(w_ref[...]))
    lane = lax.broadcasted_iota(jnp.int32, logits.shape, 1)
    logits = jnp.where(lane < n_experts, logits, -jnp.inf)
    m1 = jnp.max(logits, axis=1, keepdims=True)
    i1 = jnp.min(jnp.where(logits == m1, lane, HEAD), axis=1, keepdims=True)
    rest = jnp.where(lane == i1, -jnp.inf, logits)
    m2 = jnp.max(rest, axis=1, keepdims=True)
    i2 = jnp.min(jnp.where(rest == m2, lane, HEAD), axis=1, keepdims=True)
    e2 = jnp.exp(m2 - m1)
    idx_ref[...] = jnp.where(lane == 0, i1, i2)
    wgt_ref[...] = jnp.where(lane == 0, 1.0 / (1.0 + e2), e2 / (1.0 + e2))


def _router(x, gain, w_router):
    t, d = x.shape
    n_experts = w_router.shape[1]
    tm = _pick(t, (256, 128, 64, 8))
    w_pad = jnp.zeros((d, HEAD), F32).at[:, :n_experts].set(w_router.astype(F32))
    return pl.pallas_call(
        functools.partial(_router_kernel, n_experts=n_experts),
        grid=(t // tm,),
        in_specs=[pl.BlockSpec((tm, d), lambda i: (i, 0)),
                  pl.BlockSpec((1, d), lambda i: (0, 0)),
                  pl.BlockSpec((d, HEAD), lambda i: (0, 0))],
        out_specs=[pl.BlockSpec((tm, HEAD), lambda i: (i, 0)),
                   pl.BlockSpec((tm, HEAD), lambda i: (i, 0))],
        out_shape=[jax.ShapeDtypeStruct((t, HEAD), jnp.int32),
                   jax.ShapeDtypeStruct((t, HEAD), F32)],
        compiler_params=_params(1),
        name="router",
    )(x, gain.reshape(1, d).astype(F32), w_pad)


def _routing_plan(top_idx, n_experts, tile):
    t = top_idx.shape[0]
    flat = top_idx.reshape(-1)
    onehot = (flat[:, None] == jnp.arange(n_experts)[None, :]).astype(jnp.int32)
    rank = jnp.sum((jnp.cumsum(onehot, axis=0) - onehot) * onehot, axis=1)
    counts = jnp.sum(onehot, axis=0)
    tiles_per = (counts + tile - 1) // tile
    tile_end = jnp.cumsum(tiles_per)
    dest = (tile_end - tiles_per)[flat] * tile + rank
    n_tiles = (TOP_K * t) // tile + n_experts
    tile_ids = jnp.arange(n_tiles)
    tile_expert = jnp.minimum(jnp.sum((tile_ids[:, None] >= tile_end[None, :]).astype(jnp.int32),
                                      axis=1), n_experts - 1).astype(jnp.int32)
    tile_active = (tile_ids < tile_end[-1]).astype(jnp.int32)
    src = jnp.zeros((n_tiles * tile,), jnp.int32).at[dest].set(
        jnp.arange(TOP_K * t, dtype=jnp.int32) // TOP_K)
    return src, dest.reshape(t, TOP_K).astype(jnp.int32), tile_expert, tile_active


def _row_copy(src_hbm, row, dst_vmem, slot, sem):
    return pltpu.make_async_copy(src_hbm.at[pl.ds(row, 1)], dst_vmem.at[pl.ds(slot, 1)], sem)


def _gather_norm_kernel(src_ref, act_ref, h_hbm, gain_ref, o_ref, buf, sem, *, tile):
    i = pl.program_id(0)
    last = pl.num_programs(0) - 1
    slot = i % 2

    def start_tile(t, s):
        @pl.when(act_ref[t] == 1)
        def _():
            def issue(p, carry):
                for q in range(2):
                    r = 2 * p + q
                    _row_copy(h_hbm, src_ref[t * tile + r], buf.at[s], r,
                              sem.at[s]).start(priority=q)
                return carry

            lax.fori_loop(0, tile // 2, issue, 0, unroll=4)

    @pl.when(i == 0)
    def _():
        start_tile(0, 0)

    @pl.when(i < last)
    def _():
        start_tile(i + 1, 1 - slot)

    @pl.when(act_ref[i] == 1)
    def _():
        def drain(r, carry):
            _row_copy(h_hbm, 0, buf.at[slot], r, sem.at[slot]).wait()
            return carry

        lax.fori_loop(0, tile, drain, 0, unroll=8)
        o_ref[...] = _rms(buf[slot], gain_ref[...]).astype(o_ref.dtype)

    @pl.when(act_ref[i] == 0)
    def _():
        o_ref[...] = jnp.zeros_like(o_ref)


def _gather_norm(h, gain, src, tile_active, *, tile):
    d = h.shape[1]
    n_tiles = tile_active.shape[0]
    return pl.pallas_call(
        functools.partial(_gather_norm_kernel, tile=tile),
        grid_spec=pltpu.PrefetchScalarGridSpec(
            num_scalar_prefetch=2,
            grid=(n_tiles,),
            in_specs=[pl.BlockSpec(memory_space=pl.ANY),
                      pl.BlockSpec((1, d), lambda i, s, a: (0, 0))],
            out_specs=pl.BlockSpec((tile, d), lambda i, s, a: (i, 0)),
            scratch_shapes=[pltpu.VMEM((2, tile, d), F32), pltpu.SemaphoreType.DMA((2,))]),
        out_shape=jax.ShapeDtypeStruct((n_tiles * tile, d), BF16),
        compiler_params=_params(1),
        name="moe_gather",
    )(src, tile_active, h, gain.reshape(1, d).astype(F32))


def _moe_up_kernel(te_ref, act_ref, a_ref, wg_ref, wu_ref, o_ref):
    i = pl.program_id(1)

    @pl.when(act_ref[i] == 1)
    def _():
        a = a_ref[...]
        g = _dot(a, wg_ref[...].astype(BF16))
        u = _dot(a, wu_ref[...].astype(BF16))
        o_ref[...] = (_silu(g) * u).astype(o_ref.dtype)

    @pl.when(act_ref[i] == 0)
    def _():
        o_ref[...] = jnp.zeros_like(o_ref)


def _moe_up(xg, wg, wu, tile_expert, tile_active, *, tile, tn):
    d = xg.shape[1]
    n = wg.shape[-1]
    n_tiles = tile_active.shape[0]
    w_spec = pl.BlockSpec((None, d, tn), lambda j, i, te, a: (te[i], 0, j))
    return pl.pallas_call(
        _moe_up_kernel,
        grid_spec=pltpu.PrefetchScalarGridSpec(
            num_scalar_prefetch=2,
            grid=(n // tn, n_tiles),
            in_specs=[pl.BlockSpec((tile, d), lambda j, i, te, a: (i, 0)), w_spec, w_spec],
            out_specs=pl.BlockSpec((tile, tn), lambda j, i, te, a: (i, j))),
        out_shape=jax.ShapeDtypeStruct((n_tiles * tile, n), BF16),
        compiler_params=_params(2),
        name="moe_up",
    )(tile_expert, tile_active, xg, wg, wu)


def _moe_down_kernel(te_ref, act_ref, a_ref, w_ref, o_ref):
    i = pl.program_id(1)

    @pl.when(act_ref[i] == 1)
    def _():
        o_ref[...] = _dot(a_ref[...], w_ref[...].astype(BF16))

    @pl.when(act_ref[i] == 0)
    def _():
        o_ref[...] = jnp.zeros_like(o_ref)


def _moe_down(act, wd, tile_expert, tile_active, *, tile, tn):
    kd = act.shape[1]
    n = wd.shape[-1]
    n_tiles = tile_active.shape[0]
    return pl.pallas_call(
        _moe_down_kernel,
        grid_spec=pltpu.PrefetchScalarGridSpec(
            num_scalar_prefetch=2,
            grid=(n // tn, n_tiles),
            in_specs=[pl.BlockSpec((tile, kd), lambda j, i, te, a: (i, 0)),
                      pl.BlockSpec((None, kd, tn), lambda j, i, te, a: (te[i], 0, j))],
            out_specs=pl.BlockSpec((tile, tn), lambda j, i, te, a: (i, j))),
        out_shape=jax.ShapeDtypeStruct((n_tiles * tile, n), F32),
        compiler_params=_params(2),
        name="moe_down",
    )(tile_expert, tile_active, act, wd)


def _combine_kernel(d0_ref, d1_ref, y_hbm, h_ref, w_ref, g_ref, o_ref, buf0, buf1, sem, *, tc,
                    final_norm):
    i = pl.program_id(0)
    slot = i % 2

    def start_tile(t, s):
        def issue(r, carry):
            _row_copy(y_hbm, d0_ref[t * tc + r], buf0.at[s], r, sem.at[s]).start(priority=0)
            _row_copy(y_hbm, d1_ref[t * tc + r], buf1.at[s], r, sem.at[s]).start(priority=1)
            return carry

        lax.fori_loop(0, tc, issue, 0, unroll=8)

    @pl.when(i == 0)
    def _():
        start_tile(0, 0)

    @pl.when(i < pl.num_programs(0) - 1)
    def _():
        start_tile(i + 1, 1 - slot)

    def drain(r, carry):
        _row_copy(y_hbm, 0, buf0.at[slot], r, sem.at[slot]).wait()
        _row_copy(y_hbm, 0, buf1.at[slot], r, sem.at[slot]).wait()
        return carry

    lax.fori_loop(0, tc, drain, 0, unroll=8)
    w = w_ref[...]
    out = h_ref[...] + (w[:, 0:1] * buf0[slot] + w[:, 1:2] * buf1[slot])
    if final_norm:
        out = _rms(out, g_ref[...])
    o_ref[...] = out


def _combine(h, yg, dest, weights, final_gain):
    t, d = h.shape
    tc = _pick(t, (256, 128, 64, 8))
    final_norm = final_gain is not None
    gain = (final_gain if final_norm else jnp.ones((d,), F32)).reshape(1, d).astype(F32)
    return pl.pallas_call(
        functools.partial(_combine_kernel, tc=tc, final_norm=final_norm),
        grid_spec=pltpu.PrefetchScalarGridSpec(
            num_scalar_prefetch=2,
            grid=(t // tc,),
            in_specs=[pl.BlockSpec(memory_space=pl.ANY),
                      pl.BlockSpec((tc, d), lambda i, a, b: (i, 0)),
                      pl.BlockSpec((tc, HEAD), lambda i, a, b: (i, 0)),
                      pl.BlockSpec((1, d), lambda i, a, b: (0, 0))],
            out_specs=pl.BlockSpec((tc, d), lambda i, a, b: (i, 0)),
            scratch_shapes=[pltpu.VMEM((2, tc, d), F32), pltpu.VMEM((2, tc, d), F32),
                            pltpu.SemaphoreType.DMA((2,))]),
        out_shape=jax.ShapeDtypeStruct((t, d), F32),
        compiler_params=_params(1),
        name="moe_combine",
    )(dest[:, 0], dest[:, 1], yg, h, weights, gain)


def _moe_tile(rows):
    return 768 if rows >= 6144 else _pick(rows, (256, 128))


def _moe(h, gain, w_router, wg, wu, wd, final_gain):
    t, d = h.shape
    n_experts = w_router.shape[1]
    d_ff = wg.shape[-1]
    tile = _moe_tile(TOP_K * t)
    top_idx, top_w = _router(h, gain, w_router)
    src, dest, tile_expert, tile_active = _routing_plan(top_idx[:, :TOP_K], n_experts, tile)
    xg = _gather_norm(h, gain, src, tile_active, tile=tile)
    act = _moe_up(xg, wg, wu, tile_expert, tile_active, tile=tile, tn=_pick(d_ff, (512, 256, 128)))
    yg = _moe_down(act, wd, tile_expert, tile_active, tile=tile, tn=_pick(d, (512, 256, 128)))
    return _combine(h, yg, dest, top_w, final_gain)


def kernel(x, w_in, conv_gdn, gdn_a_log, gdn_dt_bias, gdn_norm, hgrn_lb, hgrn_norm, rel_bias,
           w_out, norm_mix, norm_ffn, w_gate_dense, w_up_dense, w_down_dense, w_router,
           w_gate_moe, w_up_moe, w_down_moe, norm_final):
    bsz, seq, d = x.shape
    depth = w_in.shape[0]
    t = bsz * seq
    n_gdn = gdn_a_log.shape[1]
    n_hgrn = hgrn_lb.shape[1] // HEAD
    n_q = rel_bias.shape[1]
    n_kv = n_q // len(DIL_PAIRS)

    split = 4 * n_gdn * HEAD
    shift = 2 * n_gdn
    rest = w_in.shape[-1] - split - shift
    tn_a = _pick(split, (512, 256, 128))
    tn_b = _pick(math.gcd(rest, split), (512, 256, 128))
    col_hq = 0
    col_dq = 4 * n_hgrn
    col_dk = col_dq + n_q
    col_dv = col_dk + n_kv

    w_in_t = jnp.swapaxes(w_in, 1, 2)
    bias_tbl = _dil_bias_tables(rel_bias)
    tm = _pick(t, (1024, 512, 256, 128))
    final_done = False

    h = x.reshape(t, d).astype(F32)
    for layer in range(depth):
        hn = _rmsnorm(h, norm_mix[layer], BF16)
        proj_a = _proj_t(hn, w_in_t, layer, row0=0, shift=0, width=split + tn_a, tm=tm, tn=tn_a)
        proj_b = _proj_t(hn, w_in_t, layer, row0=split, shift=shift, width=rest, tm=tm, tn=tn_b)
        oa = _gdn(proj_a, conv_gdn[layer].astype(F32), gdn_a_log[layer], gdn_dt_bias[layer],
                  gdn_norm[layer], bsz=bsz, seq=seq, n_heads=n_gdn, col_q=0, col_z=3 * n_gdn,
                  col_tail=split // HEAD)
        ob = _hgrn(proj_b, hgrn_lb, hgrn_norm[layer], bsz=bsz, seq=seq, n_heads=n_hgrn,
                   col_q=col_hq, layer=layer)
        oc = _dilated(proj_b, bias_tbl, bsz=bsz, seq=seq, n_kv=n_kv, col_q=col_dq, col_k=col_dk,
                      col_v=col_dv)
        h = _out_proj([oa, ob, oc], w_out[layer], h, tm=tm, tn=_pick(d, (512, 256, 128)))

        idx = layer // 2
        if layer % 2 == 0:
            hn = _rmsnorm(h, norm_ffn[layer], BF16)
            d_ff = w_gate_dense.shape[-1]
            act = _swiglu_up(hn, w_gate_dense[idx], w_up_dense[idx], tm=tm,
                             tn=_pick(d_ff, (256, 128)))
            h = _matmul(act, w_down_dense[idx], tm=_pick(t, (2048, 1024, 512, 256, 128)),
                        tn=_pick(d, (1024, 512, 256, 128)),
                        tk=_pick(d_ff, (1024, 896, 512, 256, 128)), residual=h, name="ffn_down")
        else:
            final_done = layer == depth - 1
            h = _moe(h, norm_ffn[layer], w_router[idx], w_gate_moe[idx], w_up_moe[idx],
                     w_down_moe[idx], norm_final if final_done else None)

    out = h if final_done else _rmsnorm(h, norm_final, F32)
    return out.reshape(bsz, seq, d).astype(x.dtype)
```

```python
import functools
import math

import jax
import jax.numpy as jnp
from jax import lax
from jax.experimental import pallas as pl
from jax.experimental.pallas import tpu as pltpu

F32 = jnp.float32
BF16 = jnp.bfloat16

HEAD = 128
CHUNK = 64
SUB = 8
CONV_WIDTH = 4
DIL_PAIRS = ((128, 1), (512, 4), (2048, 16))
DIL_BLK = 128
DIL_SUPER = 2048
DIL_UNITS = 4
NUM_BUCKETS = 32
MAX_DISTANCE = 2048
TOP_K = 2
RMS_EPS = 1e-6
MASK_VALUE = -1e30
MIN_GATE = 1e-20
VMEM_LIMIT = 56 * 1024 * 1024


def _params(n_axes):
    return pltpu.CompilerParams(dimension_semantics=("arbitrary",) * n_axes,
                                vmem_limit_bytes=VMEM_LIMIT)


def _pick(n, cands):
    for c in cands:
        if n % c == 0:
            return c
    return n


def _dot(a, b):
    return jnp.dot(a, b, preferred_element_type=F32)


def _mm(a, b):
    return _dot(a.astype(BF16), b.astype(BF16))


def _mm_nt(a, b):
    return lax.dot_general(a.astype(BF16), b.astype(BF16), (((1,), (1,)), ((), ())),
                           preferred_element_type=F32)


def _mm_tn(a, b):
    return lax.dot_general(a.astype(BF16), b.astype(BF16), (((0,), (0,)), ((), ())),
                           preferred_element_type=F32)


def _split(x):
    hi = x.astype(BF16)
    return hi, (x - hi.astype(F32)).astype(BF16)


def _mm_split(a, b):
    return _dot(a[0], b[0]) + (_dot(a[0], b[1]) + _dot(a[1], b[0]))


def _sigmoid(x):
    return 1.0 / (1.0 + jnp.exp(-x))


def _silu(x):
    return x * _sigmoid(x)


def _softplus(x):
    return jnp.maximum(x, 0.0) + jnp.log(1.0 + jnp.exp(-jnp.abs(x)))


def _rms(x, gain):
    return x * lax.rsqrt(jnp.mean(x * x, axis=-1, keepdims=True) + RMS_EPS) * gain


def _chunk_cumsum(x, rin):
    s = 1
    while s < CHUNK:
        x = x + jnp.where(rin >= s, pltpu.roll(x, s, 0), 0.0)
        s *= 2
    return x


def _rmsnorm_kernel(x_ref, g_ref, o_ref):
    o_ref[...] = _rms(x_ref[...], g_ref[...]).astype(o_ref.dtype)


def _rmsnorm(x, gain, out_dtype):
    t, d = x.shape
    tm = _pick(t, (256, 128, 64, 8))
    return pl.pallas_call(
        _rmsnorm_kernel,
        grid=(t // tm,),
        in_specs=[pl.BlockSpec((tm, d), lambda i: (i, 0)),
                  pl.BlockSpec((1, d), lambda i: (0, 0))],
        out_specs=pl.BlockSpec((tm, d), lambda i: (i, 0)),
        out_shape=jax.ShapeDtypeStruct((t, d), out_dtype),
        compiler_params=_params(1),
        name="rmsnorm",
    )(x, gain.reshape(1, d).astype(F32))


def _matmul_kernel(*refs, nk, has_res):
    a_ref, b_ref = refs[0], refs[1]
    r_ref = refs[2] if has_res else None
    o_ref = refs[-1]
    part = _dot(a_ref[...], b_ref[...].astype(BF16))

    def first():
        return part if r_ref is None else r_ref[...] + part

    if nk == 1:
        o_ref[...] = first()
        return
    @pl.when(pl.program_id(2) == 0)
    def _():
        o_ref[...] = r_ref[...] if r_ref is not None else jnp.zeros_like(o_ref)

    o_ref[...] += part


def _matmul(a, b, *, tm, tn, tk, residual=None, name="matmul"):
    m, kd = a.shape
    n = b.shape[-1]
    nk = kd // tk
    in_specs = [pl.BlockSpec((tm, tk), lambda i, j, k: (i, k)),
                pl.BlockSpec((tk, tn), lambda i, j, k: (k, j))]
    args = [a, b]
    if residual is not None:
        in_specs.append(pl.BlockSpec((tm, tn), lambda i, j, k: (i, j),
                                     pipeline_mode=pl.Buffered(1)))
        args.append(residual)
    return pl.pallas_call(
        functools.partial(_matmul_kernel, nk=nk, has_res=residual is not None),
        grid=(m // tm, n // tn, nk),
        in_specs=in_specs,
        out_specs=pl.BlockSpec((tm, tn), lambda i, j, k: (i, j)),
        out_shape=jax.ShapeDtypeStruct((m, n), F32),
        compiler_params=_params(3),
        name=name,
    )(*args)


def _proj_t_kernel(*refs, shift, chunk):
    a_ref, wm_ref = refs[0], refs[1]
    wx_ref = refs[2] if shift else None
    o_ref, w_scr = refs[-2], refs[-1]
    kd, tn = w_scr.shape

    @pl.when(pl.program_id(1) == 0)
    def _():
        for r in range(kd // chunk):
            cs = slice(r * chunk, (r + 1) * chunk)
            blk = wm_ref[:, cs]
            if shift:
                cat = jnp.concatenate([blk, wx_ref[:, cs]], axis=0)
                blk = pltpu.roll(cat, tn + HEAD - shift, 0)[:tn]
            w_scr[cs, :] = blk.T.astype(BF16)

    o_ref[...] = _dot(a_ref[...], w_scr[...])


def _proj_t(a, wt, lead, *, row0, shift, width, tm, tn):
    m, kd = a.shape
    in_specs = [pl.BlockSpec((tm, kd), lambda j, i: (i, 0)),
                pl.BlockSpec((None, tn, kd), lambda j, i: (lead, row0 // tn + j, 0))]
    args = [a, wt]
    if shift:
        in_specs.append(pl.BlockSpec((None, HEAD, kd),
                                     lambda j, i: (lead, (row0 + tn * (j + 1)) // HEAD, 0)))
        args.append(wt)
    return pl.pallas_call(
        functools.partial(_proj_t_kernel, shift=shift, chunk=_pick(kd, (512, 256, 128))),
        grid=(width // tn, m // tm),
        in_specs=in_specs,
        out_specs=pl.BlockSpec((tm, tn), lambda j, i: (i, j)),
        out_shape=jax.ShapeDtypeStruct((m, width), F32),
        scratch_shapes=[pltpu.VMEM((kd, tn), BF16)],
        compiler_params=_params(2),
        name="in_proj",
    )(*args)


def _swiglu_up_kernel(a_ref, wg_ref, wu_ref, o_ref):
    a = a_ref[...]
    g = _dot(a, wg_ref[...].astype(BF16))
    u = _dot(a, wu_ref[...].astype(BF16))
    o_ref[...] = (_silu(g) * u).astype(o_ref.dtype)


def _swiglu_up(a, wg, wu, *, tm, tn):
    m, kd = a.shape
    n = wg.shape[-1]
    w_spec = pl.BlockSpec((kd, tn), lambda i, j: (0, j))
    return pl.pallas_call(
        _swiglu_up_kernel,
        grid=(m // tm, n // tn),
        in_specs=[pl.BlockSpec((tm, kd), lambda i, j: (i, 0)), w_spec, w_spec],
        out_specs=pl.BlockSpec((tm, tn), lambda i, j: (i, j)),
        out_shape=jax.ShapeDtypeStruct((m, n), BF16),
        compiler_params=_params(2),
        name="swiglu_up",
    )(a, wg, wu)


def _out_proj_kernel(*refs, n_pieces):
    a_refs = refs[:n_pieces]
    w_refs = refs[n_pieces:2 * n_pieces]
    r_ref, o_ref = refs[2 * n_pieces], refs[2 * n_pieces + 1]
    acc = r_ref[...]
    for a_ref, w_ref in zip(a_refs, w_refs):
        acc = acc + _dot(a_ref[...], w_ref[...].astype(BF16))
    o_ref[...] = acc


def _out_proj(parts, w, lead, residual, *, tm, tn):
    m = residual.shape[0]
    n = w.shape[-1]
    unit = HEAD * functools.reduce(math.gcd, [p.shape[1] // HEAD for p in parts])
    a_specs, a_args = [], []
    for p in parts:
        for c in range(p.shape[1] // unit):
            a_specs.append(pl.BlockSpec((tm, unit), lambda i, j, c=c: (i, c)))
            a_args.append(p)
    n_pieces = len(a_args)
    w_specs = [pl.BlockSpec((None, unit, tn), lambda i, j, c=c: (lead, c, j))
               for c in range(n_pieces)]
    return pl.pallas_call(
        functools.partial(_out_proj_kernel, n_pieces=n_pieces),
        grid=(m // tm, n // tn),
        in_specs=a_specs + w_specs + [pl.BlockSpec((tm, tn), lambda i, j: (i, j))],
        out_specs=pl.BlockSpec((tm, tn), lambda i, j: (i, j)),
        out_shape=jax.ShapeDtypeStruct((m, n), F32),
        compiler_params=_params(2),
        name="out_proj",
    )(*a_args, *([w] * n_pieces), residual)


def _gdn_kernel(alog_ref, dtb_ref, q_ref, qh_ref, k_ref, kh_ref, v_ref, vh_ref, z_ref, t_ref,
                wq_ref, wk_ref, wv_ref, gain_ref, o_ref, s_ref, *, tb, n_heads, group):
    hg = pl.program_id(1)
    first = pl.program_id(2) == 0

    @pl.when(first)
    def _():
        s_ref[...] = jnp.zeros_like(s_ref)

    def conv_silu(x_ref, xh_ref, w_ref):
        halo = jnp.where(first, 0.0, xh_ref[...])
        ext = jnp.concatenate([halo, x_ref[...]], axis=0)
        w = w_ref[...]
        y = pltpu.roll(ext, 3, 0)[SUB:] * w[0:1]
        y = y + pltpu.roll(ext, 2, 0)[SUB:] * w[1:2]
        y = y + pltpu.roll(ext, 1, 0)[SUB:] * w[2:3]
        y = y + ext[SUB:] * w[3:4]
        return _silu(y)

    def l2norm(x):
        return x * lax.rsqrt(jnp.sum(x * x, axis=-1, keepdims=True) + RMS_EPS)

    q_all = conv_silu(q_ref, qh_ref, wq_ref)
    k_all = conv_silu(k_ref, kh_ref, wk_ref)
    v_all = conv_silu(v_ref, vh_ref, wv_ref)
    tail = t_ref[...]
    lane = lax.broadcasted_iota(jnp.int32, (tb, HEAD), 1)
    rin = lax.broadcasted_iota(jnp.int32, (tb, HEAD), 0) & (CHUNK - 1)

    row = lax.broadcasted_iota(jnp.int32, (CHUNK, CHUNK), 0)
    col = lax.broadcasted_iota(jnp.int32, (CHUNK, CHUNK), 1)
    eye = (row == col).astype(F32)
    causal = row >= col
    strict = row > col
    level_masks = [(((row >> k) == (col >> k)) & ((row >> (k - 1)) != (col >> (k - 1)))).astype(F32)
                   for k in range(1, 7)]

    n_chunks = tb // CHUNK
    units = []
    for g in range(group):
        h = hg * group + g
        hs = slice(g * HEAD, (g + 1) * HEAD)
        q_h = l2norm(q_all[:, hs]) * (HEAD ** -0.5)
        k_h = l2norm(k_all[:, hs])
        v_h = v_all[:, hs]
        b_col = jnp.sum(jnp.where(lane == h, tail, 0.0), axis=1, keepdims=True)
        a_col = jnp.sum(jnp.where(lane == h + n_heads, tail, 0.0), axis=1, keepdims=True)
        decay_rate = jnp.exp(jnp.zeros((1, 1), F32) + alog_ref[h])
        log_decay = -decay_rate * _softplus(a_col + dtb_ref[h])
        beta_h = jnp.broadcast_to(_sigmoid(b_col), (tb, HEAD))
        gc_h = _chunk_cumsum(jnp.broadcast_to(log_decay, (tb, HEAD)), rin)
        for c in range(n_chunks):
            sl = slice(c * CHUNK, (c + 1) * CHUNK)
            units.append(dict(q=q_h[sl], k=k_h[sl], v=v_h[sl], beta=beta_h[sl], gc=gc_h[sl]))

    for p in units:
        gcol = p["gc"][:, :CHUNK]
        grow = jnp.sum(gcol * eye, axis=0, keepdims=True)
        p["decay"] = jnp.where(causal, jnp.exp(jnp.where(causal, gcol - grow, 0.0)), 0.0)
        p["egc"] = jnp.exp(p["gc"])
        p["kb"] = p["k"] * p["beta"]
    for p in units:
        p["lower"] = jnp.where(strict, _mm_nt(p["kb"], p["k"]) * p["decay"], 0.0)
        p["attn"] = _mm_nt(p["q"], p["k"]) * p["decay"]
        p["inv"] = eye - p["lower"] * level_masks[0]
    for lvl in range(1, 6):
        for p in units:
            p["inv_s"] = _split(p["inv"])
            p["step"] = _mm_split(p["inv_s"], _split(p["lower"] * level_masks[lvl]))
        for p in units:
            p["inv"] = p["inv"] - _mm_split(_split(p["step"]), p["inv_s"])
    for p in units:
        inv_s = _split(p["inv"])
        p["u"] = _mm_split(inv_s, _split(p["v"] * p["beta"]))
        p["w"] = _mm_split(inv_s, _split(p["kb"] * p["egc"]))
        g_last = p["gc"][CHUNK - 1:CHUNK, :]
        p["q_dec"] = p["q"] * p["egc"]
        p["k_dec"] = p["k"] * jnp.exp(g_last - p["gc"])
        p["e_last"] = jnp.exp(g_last)

    gain = gain_ref[...]
    states = [s_ref[g] for g in range(group)]
    for c in range(n_chunks):
        ps = [units[g * n_chunks + c] for g in range(group)]
        v_new = [p["u"] - _mm(p["w"], s) for p, s in zip(ps, states)]
        outs = [_mm(p["q_dec"], s) + _mm(p["attn"], vn) for p, s, vn in zip(ps, states, v_new)]
        states = [s * p["e_last"] + _mm_tn(p["k_dec"], vn) for p, s, vn in zip(ps, states, v_new)]
        sl = slice(c * CHUNK, (c + 1) * CHUNK)
        for g, o in enumerate(outs):
            hs = slice(g * HEAD, (g + 1) * HEAD)
            o_ref[sl, hs] = (_rms(o, gain) * _silu(z_ref[sl, hs])).astype(o_ref.dtype)
    for g in range(group):
        s_ref[g] = states[g]


def _gdn(proj, conv_w, a_log, dt_bias, gain, *, bsz, seq, n_heads, col_q, col_z, col_tail):
    group = _pick(n_heads, (4, 3, 2))
    tb = _pick(seq, (256, 128, 64))
    nb = seq // tb
    width = group * HEAD

    def main(col):
        return pl.BlockSpec((tb, width), lambda b, h, i: (b * nb + i, col // group + h))

    def halo(col):
        return pl.BlockSpec(
            (SUB, width),
            lambda b, h, i: (jnp.maximum(b * (seq // SUB) + i * (tb // SUB) - 1, 0),
                             col // group + h))

    def wspec(col):
        return pl.BlockSpec((CONV_WIDTH, width), lambda b, h, i: (0, col // group + h))

    smem = pl.BlockSpec(memory_space=pltpu.SMEM)
    col_k = col_q + n_heads
    col_v = col_q + 2 * n_heads
    assert all(c % group == 0 for c in (col_q, col_k, col_v, col_z))
    return pl.pallas_call(
        functools.partial(_gdn_kernel, tb=tb, n_heads=n_heads, group=group),
        grid=(bsz, n_heads // group, nb),
        in_specs=[smem, smem,
                  main(col_q), halo(col_q), main(col_k), halo(col_k), main(col_v), halo(col_v),
                  main(col_z),
                  pl.BlockSpec((tb, HEAD), lambda b, h, i: (b * nb + i, col_tail)),
                  wspec(0), wspec(n_heads), wspec(2 * n_heads),
                  pl.BlockSpec((1, HEAD), lambda b, h, i: (0, 0))],
        out_specs=pl.BlockSpec((tb, width), lambda b, h, i: (b * nb + i, h)),
        out_shape=jax.ShapeDtypeStruct((bsz * seq, n_heads * HEAD), BF16),
        scratch_shapes=[pltpu.VMEM((group, HEAD, HEAD), F32)],
        compiler_params=_params(3),
        name="gdn",
    )(a_log.astype(F32), dt_bias.astype(F32), proj, proj, proj, proj, proj, proj, proj, proj,
      conv_w, conv_w, conv_w, gain.reshape(1, HEAD).astype(F32))


def _hgrn_kernel(q_ref, f_ref, i_ref, g_ref, lb_ref, gain_ref, o_ref, s_ref, *, tb, layer, group):
    @pl.when(pl.program_id(2) == 0)
    def _():
        s_ref[...] = jnp.zeros_like(s_ref)

    lbp = lb_ref[...]
    e = jnp.exp(lbp - jnp.max(lbp, axis=0, keepdims=True))
    probs = e / jnp.sum(e, axis=0, keepdims=True)
    lower = jnp.sum(probs[:layer + 1], axis=0, keepdims=True) - probs[0:1]

    f_pre = f_ref[...]
    f_gate = lower + (1.0 - lower) * _sigmoid(f_pre)
    log_f = jnp.log(jnp.maximum(f_gate, MIN_GATE))
    rin = lax.broadcasted_iota(jnp.int32, f_pre.shape, 0) & (CHUNK - 1)
    b_all = _chunk_cumsum(log_f, rin)
    k_all = (1.0 - lower) * _sigmoid(-f_pre)
    q_all = _silu(q_ref[...])
    v_all = i_ref[...]

    row = lax.broadcasted_iota(jnp.int32, (CHUNK, CHUNK), 0)
    col = lax.broadcasted_iota(jnp.int32, (CHUNK, CHUNK), 1)
    sub_row = lax.broadcasted_iota(jnp.int32, (SUB, 1), 0)
    off_masks = {1 << sh: ((((row >> sh) & 1) == 1) & ((col >> sh) == (row >> sh) - 1)).astype(F32)
                 for sh in (3, 4, 5)}
    gain = gain_ref[...]
    n_sub = CHUNK // SUB
    n_chunks = tb // CHUNK

    units = []
    for g in range(group):
        hs = slice(g * HEAD, (g + 1) * HEAD)
        for c in range(n_chunks):
            sl = slice(c * CHUNK, (c + 1) * CHUNK)
            units.append(dict(q=q_all[sl, hs], k=k_all[sl, hs], b=b_all[sl, hs], v=v_all[sl, hs]))

    for p in units:
        q, k, b = p["q"], p["k"], p["b"]
        b_last = b[CHUNK - 1:CHUNK, :]
        p["q_dec"] = q * jnp.exp(b)
        p["e_last"] = jnp.exp(b_last)
        p["kv"] = _mm_tn(p["v"], k * jnp.exp(b_last - b))
    for m in (8, 16, 32):
        for p in units:
            q, k, b = p["q"], p["k"], p["b"]
            ref_q, ref_k = [], []
            for blk in range(n_sub):
                start = (blk * SUB // m) * m
                ref_q.append(jnp.broadcast_to(b[start:start + 1, :], (SUB, HEAD)))
                nxt = start + m
                if nxt < CHUNK:
                    ref_k.append(jnp.broadcast_to(b[nxt:nxt + 1, :], (SUB, HEAD)))
                else:
                    ref_k.append(b[blk * SUB:(blk + 1) * SUB, :])
            qe = q * jnp.exp(b - jnp.concatenate(ref_q, axis=0))
            ke = k * jnp.exp(jnp.concatenate(ref_k, axis=0) - b)
            part = _mm_nt(qe, ke) * off_masks[m]
            p["scores"] = part if m == 8 else p["scores"] + part
    for p in units:
        p["intra"] = _mm(p["scores"], p["v"])
    for p in units:
        q, k, b, v = p["q"], p["k"], p["b"], p["v"]
        diag = []
        for blk in range(n_sub):
            sl = slice(blk * SUB, (blk + 1) * SUB)
            qi, ki, bi, vi = q[sl], k[sl], b[sl], v[sl]
            acc = jnp.zeros((SUB, HEAD), F32)
            for s in range(SUB):
                m = sub_row >= s
                dec = jnp.where(m, jnp.exp(jnp.where(m, bi - bi[s:s + 1], 0.0)), 0.0)
                wgt = jnp.sum(qi * ki[s:s + 1] * dec, axis=1, keepdims=True)
                acc = acc + wgt * vi[s:s + 1]
            diag.append(acc)
        p["intra"] = p["intra"] + jnp.concatenate(diag, axis=0)

    for g in range(group):
        hs = slice(g * HEAD, (g + 1) * HEAD)
        state_t = s_ref[g]
        for c in range(n_chunks):
            p = units[g * n_chunks + c]
            o = p["intra"] + _mm_nt(p["q_dec"], state_t)
            state_t = state_t * p["e_last"] + p["kv"]
            sl = slice(c * CHUNK, (c + 1) * CHUNK)
            o_ref[sl, hs] = (_rms(o, gain) * _silu(g_ref[sl, hs])).astype(o_ref.dtype)
        s_ref[g] = state_t


def _hgrn(proj, hgrn_lb, gain, *, bsz, seq, n_heads, col_q, layer):
    group = _pick(n_heads, (4, 3, 2))
    tb = _pick(seq, (256, 128, 64))
    nb = seq // tb
    depth = hgrn_lb.shape[0]
    width = group * HEAD
    assert col_q % group == 0

    def main(col):
        return pl.BlockSpec((tb, width), lambda b, h, i: (b * nb + i, col // group + h))

    return pl.pallas_call(
        functools.partial(_hgrn_kernel, tb=tb, layer=layer, group=group),
        grid=(bsz, n_heads // group, nb),
        in_specs=[main(col_q), main(col_q + n_heads), main(col_q + 2 * n_heads),
                  main(col_q + 3 * n_heads),
                  pl.BlockSpec((depth, width), lambda b, h, i: (0, h)),
                  pl.BlockSpec((1, HEAD), lambda b, h, i: (0, 0))],
        out_specs=pl.BlockSpec((tb, width), lambda b, h, i: (b * nb + i, h)),
        out_shape=jax.ShapeDtypeStruct((bsz * seq, n_heads * HEAD), BF16),
        scratch_shapes=[pltpu.VMEM((group, HEAD, HEAD), F32)],
        compiler_params=_params(3),
        name="hgrn2",
    )(proj, proj, proj, proj, hgrn_lb.astype(F32), gain.reshape(1, HEAD).astype(F32))


def _dil_kernel(q0_ref, q1_ref, q2_ref, kp_ref, kc_ref, vp_ref, vc_ref,
                b0_ref, b1_ref, b2_ref, o_ref, o_scr, l_scr):
    first_super = pl.program_id(2) == 0
    q_refs = (q0_ref, q1_ref, q2_ref)
    b_refs = (b0_ref, b1_ref, b2_ref)
    qi = lax.broadcasted_iota(jnp.int32, (DIL_BLK, DIL_BLK), 0)
    ki = lax.broadcasted_iota(jnp.int32, (DIL_BLK, DIL_BLK), 1)
    valid_prev = ki >= qi
    valid_cur = ki <= qi
    scale = HEAD ** -0.5

    def rows(ref, start, dil):
        if dil == 1:
            return ref[pl.ds(start, DIL_BLK), :]
        return ref[pl.ds(start, DIL_BLK, stride=dil), :]

    for g, (window, dil) in enumerate(DIL_PAIRS):
        span = DIL_BLK * dil
        bias_p = b_refs[g][:, :DIL_BLK]
        bias_c = b_refs[g][:, DIL_BLK:]
        starts = [(s, s * span + c) for s in range(DIL_SUPER // span) for c in range(dil)]
        for u0 in range(0, len(starts), DIL_UNITS):
            units = []
            for s, start in starts[u0:u0 + DIL_UNITS]:
                before = (kp_ref, vp_ref, DIL_SUPER - span + start) if s == 0 else \
                    (kc_ref, vc_ref, start - span)
                units.append(dict(
                    s=s, start=start, q=rows(q_refs[g], start, dil),
                    k_cur=rows(kc_ref, start, dil), v_cur=rows(vc_ref, start, dil),
                    k_prev=rows(before[0], before[2], dil), v_prev=rows(before[1], before[2], dil)))
            for p in units:
                lp = jnp.where(valid_prev, _mm_nt(p["q"], p["k_prev"]) * scale + bias_p, MASK_VALUE)
                if p["s"] == 0:
                    lp = jnp.where(first_super, MASK_VALUE, lp)
                p["lp"] = lp
                p["lc"] = jnp.where(valid_cur, _mm_nt(p["q"], p["k_cur"]) * scale + bias_c,
                                    MASK_VALUE)
            for p in units:
                p["mx"] = jnp.maximum(jnp.max(p["lp"], axis=1, keepdims=True),
                                      jnp.max(p["lc"], axis=1, keepdims=True))
            for p in units:
                p["pp"] = jnp.exp(p["lp"] - p["mx"])
                p["pc"] = jnp.exp(p["lc"] - p["mx"])
                p["den"] = (jnp.sum(p["pp"], axis=1, keepdims=True)
                            + jnp.sum(p["pc"], axis=1, keepdims=True))
            for p in units:
                out = (_mm(p["pp"], p["v_prev"]) + _mm(p["pc"], p["v_cur"])) / p["den"]
                log_den = jnp.broadcast_to(p["mx"] + jnp.log(p["den"]), (DIL_BLK, HEAD))
                if dil == 1:
                    dst = pl.ds(p["start"], DIL_BLK)
                else:
                    dst = pl.ds(p["start"], DIL_BLK, stride=dil)
                o_scr[g, dst, :] = out
                l_scr[g, dst, :] = log_den

    l0, l1, l2 = l_scr[0], l_scr[1], l_scr[2]
    mx = jnp.maximum(jnp.maximum(l0, l1), l2)
    w0, w1, w2 = jnp.exp(l0 - mx), jnp.exp(l1 - mx), jnp.exp(l2 - mx)
    mixed = (w0 * o_scr[0] + w1 * o_scr[1] + w2 * o_scr[2]) / (w0 + w1 + w2)
    o_ref[...] = mixed.astype(o_ref.dtype)


def _t5_bucket(dist):
    max_exact = NUM_BUCKETS // 2
    d = jnp.maximum(dist, 1).astype(F32)
    large = max_exact + (jnp.log(d / max_exact) / math.log(MAX_DISTANCE / max_exact)
                         * (NUM_BUCKETS - max_exact)).astype(jnp.int32)
    large = jnp.clip(large, 0, NUM_BUCKETS - 1)
    return jnp.where(dist < max_exact, dist, large)


def _dil_bias_tables(rel_bias):
    n_kv = rel_bias.shape[1] // len(DIL_PAIRS)
    qi = jnp.arange(DIL_BLK)[:, None]
    ki = jnp.arange(2 * DIL_BLK)[None, :]
    steps = jnp.clip(qi + DIL_BLK - ki, 0, DIL_BLK)
    tables = []
    for g, (_, dil) in enumerate(DIL_PAIRS):
        onehot = (_t5_bucket(steps * dil)[:, :, None] == jnp.arange(NUM_BUCKETS)).astype(F32)
        heads = rel_bias.astype(F32)[:, g * n_kv:(g + 1) * n_kv]
        tables.append(jnp.einsum("qkb,bh->hqk", onehot, heads, precision=lax.Precision.HIGHEST))
    return jnp.concatenate(tables, axis=0)


def _dilated(proj, bias_tbl, *, bsz, seq, n_kv, col_q, col_k, col_v):
    ns = seq // DIL_SUPER

    def qspec(g):
        return pl.BlockSpec((DIL_SUPER, HEAD), lambda b, h, j: (b * ns + j, col_q + g * n_kv + h))

    def cur(col):
        return pl.BlockSpec((DIL_SUPER, HEAD), lambda b, h, j: (b * ns + j, col + h))

    def prev(col):
        return pl.BlockSpec((DIL_SUPER, HEAD),
                            lambda b, h, j: (b * ns + jnp.maximum(j - 1, 0), col + h))

    def bspec(g):
        return pl.BlockSpec((None, DIL_BLK, 2 * DIL_BLK), lambda b, h, j: (g * n_kv + h, 0, 0))

    return pl.pallas_call(
        _dil_kernel,
        grid=(bsz, n_kv, ns),
        in_specs=[qspec(0), qspec(1), qspec(2), prev(col_k), cur(col_k), prev(col_v), cur(col_v),
                  bspec(0), bspec(1), bspec(2)],
        out_specs=pl.BlockSpec((DIL_SUPER, HEAD), lambda b, h, j: (b * ns + j, h)),
        out_shape=jax.ShapeDtypeStruct((bsz * seq, n_kv * HEAD), BF16),
        scratch_shapes=[pltpu.VMEM((len(DIL_PAIRS), DIL_SUPER, HEAD), F32)] * 2,
        compiler_params=_params(3),
        name="dilated_attn",
    )(proj, proj, proj, proj, proj, proj, proj, bias_tbl, bias_tbl, bias_tbl)


def _router_kernel(x_ref, g_ref, w_ref, idx_ref, wgt_ref, *, n_experts):
    hn = _rms(x_ref[...], g_ref[...])
    logits = _mm_split(_split(hn), _split(w_ref[...]))
    lane = lax.broadcasted_iota(jnp.int32, logits.shape, 1)
    logits = jnp.where(lane < n_experts, logits, -jnp.inf)
    m1 = jnp.max(logits, axis=1, keepdims=True)
    i1 = jnp.min(jnp.where(logits == m1, lane, HEAD), axis=1, keepdims=True)
    rest = jnp.where(lane == i1, -jnp.inf, logits)
    m2 = jnp.max(rest, axis=1, keepdims=True)
    i2 = jnp.min(jnp.where(rest == m2, lane, HEAD), axis=1, keepdims=True)
    e2 = jnp.exp(m2 - m1)
    idx_ref[...] = jnp.where(lane == 0, i1, i2)
    wgt_ref[...] = jnp.where(lane == 0, 1.0 / (1.0 + e2), e2 / (1.0 + e2))


def _router(x, gain, w_router):
    t, d = x.shape
    n_experts = w_router.shape[1]
    tm = _pick(t, (256, 128, 64, 8))
    w_pad = jnp.zeros((d, HEAD), F32).at[:, :n_experts].set(w_router.astype(F32))
    return pl.pallas_call(
        functools.partial(_router_kernel, n_experts=n_experts),
        grid=(t // tm,),
        in_specs=[pl.BlockSpec((tm, d), lambda i: (i, 0)),
                  pl.BlockSpec((1, d), lambda i: (0, 0)),
                  pl.BlockSpec((d, HEAD), lambda i: (0, 0))],
        out_specs=[pl.BlockSpec((tm, HEAD), lambda i: (i, 0)),
                   pl.BlockSpec((tm, HEAD), lambda i: (i, 0))],
        out_shape=[jax.ShapeDtypeStruct((t, HEAD), jnp.int32),
                   jax.ShapeDtypeStruct((t, HEAD), F32)],
        compiler_params=_params(1),
        name="router",
    )(x, gain.reshape(1, d).astype(F32), w_pad)


def _routing_plan(top_idx, n_experts, tile):
    t = top_idx.shape[0]
    flat = top_idx.reshape(-1)
    onehot = (flat[:, None] == jnp.arange(n_experts)[None, :]).astype(jnp.int32)
    rank = jnp.sum((jnp.cumsum(onehot, axis=0) - onehot) * onehot, axis=1)
    counts = jnp.sum(onehot, axis=0)
    tiles_per = (counts + tile - 1) // tile
    tile_end = jnp.cumsum(tiles_per)
    dest = (tile_end - tiles_per)[flat] * tile + rank
    n_tiles = (TOP_K * t) // tile + n_experts
    tile_ids = jnp.arange(n_tiles)
    tile_expert = jnp.minimum(jnp.sum((tile_ids[:, None] >= tile_end[None, :]).astype(jnp.int32),
                                      axis=1), n_experts - 1).astype(jnp.int32)
    tile_active = (tile_ids < tile_end[-1]).astype(jnp.int32)
    src = jnp.zeros((n_tiles * tile,), jnp.int32).at[dest].set(
        jnp.arange(TOP_K * t, dtype=jnp.int32) // TOP_K)
    return src, dest.reshape(t, TOP_K).astype(jnp.int32), tile_expert, tile_active


def _row_copy(src_hbm, row, dst_vmem, slot, sem):
    return pltpu.make_async_copy(src_hbm.at[pl.ds(row, 1)], dst_vmem.at[pl.ds(slot, 1)], sem)


def _gather_norm_kernel(src_ref, act_ref, h_hbm, gain_ref, o_ref, buf, sem, *, tile):
    i = pl.program_id(0)
    last = pl.num_programs(0) - 1
    slot = i % 2

    def start_tile(t, s):
        @pl.when(act_ref[t] == 1)
        def _():
            def issue(r, carry):
                _row_copy(h_hbm, src_ref[t * tile + r], buf.at[s], r, sem.at[s]).start()
                return carry

            lax.fori_loop(0, tile, issue, 0, unroll=8)

    @pl.when(i == 0)
    def _():
        start_tile(0, 0)

    @pl.when(i < last)
    def _():
        start_tile(i + 1, 1 - slot)

    @pl.when(act_ref[i] == 1)
    def _():
        def drain(r, carry):
            _row_copy(h_hbm, 0, buf.at[slot], r, sem.at[slot]).wait()
            return carry

        lax.fori_loop(0, tile, drain, 0, unroll=8)
        o_ref[...] = _rms(buf[slot], gain_ref[...]).astype(o_ref.dtype)

    @pl.when(act_ref[i] == 0)
    def _():
        o_ref[...] = jnp.zeros_like(o_ref)


def _gather_norm(h, gain, src, tile_active, *, tile):
    d = h.shape[1]
    n_tiles = tile_active.shape[0]
    return pl.pallas_call(
        functools.partial(_gather_norm_kernel, tile=tile),
        grid_spec=pltpu.PrefetchScalarGridSpec(
            num_scalar_prefetch=2,
            grid=(n_tiles,),
            in_specs=[pl.BlockSpec(memory_space=pl.ANY),
                      pl.BlockSpec((1, d), lambda i, s, a: (0, 0))],
            out_specs=pl.BlockSpec((tile, d), lambda i, s, a: (i, 0)),
            scratch_shapes=[pltpu.VMEM((2, tile, d), F32), pltpu.SemaphoreType.DMA((2,))]),
        out_shape=jax.ShapeDtypeStruct((n_tiles * tile, d), BF16),
        compiler_params=_params(1),
        name="moe_gather",
    )(src, tile_active, h, gain.reshape(1, d).astype(F32))


def _moe_up_kernel(te_ref, act_ref, a_ref, wg_ref, wu_ref, o_ref):
    i = pl.program_id(1)

    @pl.when(act_ref[i] == 1)
    def _():
        a = a_ref[...]
        g = _dot(a, wg_ref[...].astype(BF16))
        u = _dot(a, wu_ref[...].astype(BF16))
        o_ref[...] = (_silu(g) * u).astype(o_ref.dtype)

    @pl.when(act_ref[i] == 0)
    def _():
        o_ref[...] = jnp.zeros_like(o_ref)


def _moe_up(xg, wg, wu, tile_expert, tile_active, *, tile, tn):
    d = xg.shape[1]
    n = wg.shape[-1]
    n_tiles = tile_active.shape[0]
    w_spec = pl.BlockSpec((None, d, tn), lambda j, i, te, a: (te[i], 0, j))
    return pl.pallas_call(
        _moe_up_kernel,
        grid_spec=pltpu.PrefetchScalarGridSpec(
            num_scalar_prefetch=2,
            grid=(n // tn, n_tiles),
            in_specs=[pl.BlockSpec((tile, d), lambda j, i, te, a: (i * a[i], 0)), w_spec, w_spec],
            out_specs=pl.BlockSpec((tile, tn), lambda j, i, te, a: (i, j))),
        out_shape=jax.ShapeDtypeStruct((n_tiles * tile, n), BF16),
        compiler_params=_params(2),
        name="moe_up",
    )(tile_expert, tile_active, xg, wg, wu)


def _moe_down_kernel(te_ref, act_ref, a_ref, w_ref, o_ref):
    i = pl.program_id(1)

    @pl.when(act_ref[i] == 1)
    def _():
        o_ref[...] = _dot(a_ref[...], w_ref[...].astype(BF16))

    @pl.when(act_ref[i] == 0)
    def _():
        o_ref[...] = jnp.zeros_like(o_ref)


def _moe_down(act, wd, tile_expert, tile_active, *, tile, tn):
    kd = act.shape[1]
    n = wd.shape[-1]
    n_tiles = tile_active.shape[0]
    return pl.pallas_call(
        _moe_down_kernel,
        grid_spec=pltpu.PrefetchScalarGridSpec(
            num_scalar_prefetch=2,
            grid=(n // tn, n_tiles),
            in_specs=[pl.BlockSpec((tile, kd), lambda j, i, te, a: (i * a[i], 0)),
                      pl.BlockSpec((None, kd, tn), lambda j, i, te, a: (te[i], 0, j))],
            out_specs=pl.BlockSpec((tile, tn), lambda j, i, te, a: (i, j))),
        out_shape=jax.ShapeDtypeStruct((n_tiles * tile, n), F32),
        compiler_params=_params(2),
        name="moe_down",
    )(tile_expert, tile_active, act, wd)


def _combine_kernel(d0_ref, d1_ref, y_hbm, h_ref, w_ref, g_ref, o_ref, buf0, buf1, sem, *, tc,
                    final_norm):
    i = pl.program_id(0)
    slot = i % 2

    def start_tile(t, s):
        def issue(r, carry):
            _row_copy(y_hbm, d0_ref[t * tc + r], buf0.at[s], r, sem.at[s]).start()
            _row_copy(y_hbm, d1_ref[t * tc + r], buf1.at[s], r, sem.at[s]).start()
            return carry

        lax.fori_loop(0, tc, issue, 0, unroll=8)

    @pl.when(i == 0)
    def _():
        start_tile(0, 0)

    @pl.when(i < pl.num_programs(0) - 1)
    def _():
        start_tile(i + 1, 1 - slot)

    def drain(r, carry):
        _row_copy(y_hbm, 0, buf0.at[slot], r, sem.at[slot]).wait()
        _row_copy(y_hbm, 0, buf1.at[slot], r, sem.at[slot]).wait()
        return carry

    lax.fori_loop(0, tc, drain, 0, unroll=8)
    w = w_ref[...]
    out = h_ref[...] + (w[:, 0:1] * buf0[slot] + w[:, 1:2] * buf1[slot])
    if final_norm:
        out = _rms(out, g_ref[...])
    o_ref[...] = out


def _combine(h, yg, dest, weights, final_gain):
    t, d = h.shape
    tc = _pick(t, (256, 128, 64, 8))
    final_norm = final_gain is not None
    gain = (final_gain if final_norm else jnp.ones((d,), F32)).reshape(1, d).astype(F32)
    return pl.pallas_call(
        functools.partial(_combine_kernel, tc=tc, final_norm=final_norm),
        grid_spec=pltpu.PrefetchScalarGridSpec(
            num_scalar_prefetch=2,
            grid=(t // tc,),
            in_specs=[pl.BlockSpec(memory_space=pl.ANY),
                      pl.BlockSpec((tc, d), lambda i, a, b: (i, 0)),
                      pl.BlockSpec((tc, HEAD), lambda i, a, b: (i, 0)),
                      pl.BlockSpec((1, d), lambda i, a, b: (0, 0))],
            out_specs=pl.BlockSpec((tc, d), lambda i, a, b: (i, 0)),
            scratch_shapes=[pltpu.VMEM((2, tc, d), F32), pltpu.VMEM((2, tc, d), F32),
                            pltpu.SemaphoreType.DMA((2,))]),
        out_shape=jax.ShapeDtypeStruct((t, d), F32),
        compiler_params=_params(1),
        name="moe_combine",
    )(dest[:, 0], dest[:, 1], yg, h, weights, gain)


def _moe_tile(rows):
    return 768 if rows >= 6144 else _pick(rows, (256, 128))


def _moe(h, gain, w_router, wg, wu, wd, final_gain):
    t, d = h.shape
    n_experts = w_router.shape[1]
    d_ff = wg.shape[-1]
    tile = _moe_tile(TOP_K * t)
    top_idx, top_w = _router(h, gain, w_router)
    src, dest, tile_expert, tile_active = _routing_plan(top_idx[:, :TOP_K], n_experts, tile)
    xg = _gather_norm(h, gain, src, tile_active, tile=tile)
    act = _moe_up(xg, wg, wu, tile_expert, tile_active, tile=tile, tn=_pick(d_ff, (512, 256, 128)))
    yg = _moe_down(act, wd, tile_expert, tile_active, tile=tile, tn=_pick(d, (512, 256, 128)))
    return _combine(h, yg, dest, top_w, final_gain)


def kernel(x, w_in, conv_gdn, gdn_a_log, gdn_dt_bias, gdn_norm, hgrn_lb, hgrn_norm, rel_bias,
           w_out, norm_mix, norm_ffn, w_gate_dense, w_up_dense, w_down_dense, w_router,
           w_gate_moe, w_up_moe, w_down_moe, norm_final):
    bsz, seq, d = x.shape
    depth = w_in.shape[0]
    t = bsz * seq
    n_gdn = gdn_a_log.shape[1]
    n_hgrn = hgrn_lb.shape[1] // HEAD
    n_q = rel_bias.shape[1]
    n_kv = n_q // len(DIL_PAIRS)

    split = 4 * n_gdn * HEAD
    shift = 2 * n_gdn
    rest = w_in.shape[-1] - split - shift
    tn_a = _pick(split, (512, 256, 128))
    tn_b = _pick(math.gcd(rest, split), (512, 256, 128))
    col_hq = 0
    col_dq = 4 * n_hgrn
    col_dk = col_dq + n_q
    col_dv = col_dk + n_kv

    w_in_t = jnp.swapaxes(w_in, 1, 2)
    bias_tbl = _dil_bias_tables(rel_bias)
    tm = _pick(t, (1024, 512, 256, 128))
    final_done = False

    h = x.reshape(t, d).astype(F32)
    for layer in range(depth):
        hn = _rmsnorm(h, norm_mix[layer], BF16)
        proj_a = _proj_t(hn, w_in_t, layer, row0=0, shift=0, width=split + tn_a, tm=tm, tn=tn_a)
        proj_b = _proj_t(hn, w_in_t, layer, row0=split, shift=shift, width=rest, tm=tm, tn=tn_b)
        oa = _gdn(proj_a, conv_gdn[layer].astype(F32), gdn_a_log[layer], gdn_dt_bias[layer],
                  gdn_norm[layer], bsz=bsz, seq=seq, n_heads=n_gdn, col_q=0, col_z=3 * n_gdn,
                  col_tail=split // HEAD)
        ob = _hgrn(proj_b, hgrn_lb, hgrn_norm[layer], bsz=bsz, seq=seq, n_heads=n_hgrn,
                   col_q=col_hq, layer=layer)
        oc = _dilated(proj_b, bias_tbl, bsz=bsz, seq=seq, n_kv=n_kv, col_q=col_dq, col_k=col_dk,
                      col_v=col_dv)
        h = _out_proj([oa, ob, oc], w_out, layer, h, tm=tm, tn=_pick(d, (512, 256, 128)))

        idx = layer // 2
        if layer % 2 == 0:
            hn = _rmsnorm(h, norm_ffn[layer], BF16)
            d_ff = w_gate_dense.shape[-1]
            act = _swiglu_up(hn, w_gate_dense[idx], w_up_dense[idx], tm=tm,
                             tn=_pick(d_ff, (256, 128)))
            h = _matmul(act, w_down_dense[idx], tm=tm, tn=_pick(d, (1024, 512, 256, 128)),
                        tk=_pick(d_ff, (2048, 1792, 1024, 512, 256, 128)), residual=h,
                        name="ffn_down")
        else:
            final_done = layer == depth - 1
            h = _moe(h, norm_ffn[layer], w_router[idx], w_gate_moe[idx], w_up_moe[idx],
                     w_down_moe[idx], norm_final if final_done else None)

    out = h if final_done else _rmsnorm(h, norm_final, F32)
    return out.reshape(bsz, seq, d).astype(x.dtype)
```

```python
import functools
import math

import jax
import jax.numpy as jnp
from jax import lax
from jax.experimental import pallas as pl
from jax.experimental.pallas import tpu as pltpu

F32 = jnp.float32
BF16 = jnp.bfloat16

HEAD = 128
CHUNK = 64
SUB = 8
CONV_WIDTH = 4
DIL_PAIRS = ((128, 1), (512, 4), (2048, 16))
DIL_BLK = 128
DIL_SUPER = 2048
DIL_UNITS = 4
NUM_BUCKETS = 32
MAX_DISTANCE = 2048
TOP_K = 2
RMS_EPS = 1e-6
MASK_VALUE = -1e30
MIN_GATE = 1e-20
VMEM_LIMIT = 56 * 1024 * 1024


def _params(n_axes):
    return pltpu.CompilerParams(dimension_semantics=("arbitrary",) * n_axes,
                                vmem_limit_bytes=VMEM_LIMIT)


def _pick(n, cands):
    for c in cands:
        if n % c == 0:
            return c
    return n


def _dot(a, b):
    return jnp.dot(a, b, preferred_element_type=F32)


def _mm(a, b):
    return _dot(a.astype(BF16), b.astype(BF16))


def _mm_nt(a, b):
    return lax.dot_general(a.astype(BF16), b.astype(BF16), (((1,), (1,)), ((), ())),
                           preferred_element_type=F32)


def _mm_tn(a, b):
    return lax.dot_general(a.astype(BF16), b.astype(BF16), (((0,), (0,)), ((), ())),
                           preferred_element_type=F32)


def _split(x):
    hi = x.astype(BF16)
    return hi, (x - hi.astype(F32)).astype(BF16)


def _mm_split(a, b):
    return _dot(a[0], b[0]) + (_dot(a[0], b[1]) + _dot(a[1], b[0]))


def _sigmoid(x):
    return 1.0 / (1.0 + jnp.exp(-x))


def _silu(x):
    return x * _sigmoid(x)


def _softplus(x):
    return jnp.maximum(x, 0.0) + jnp.log(1.0 + jnp.exp(-jnp.abs(x)))


def _rms(x, gain):
    return x * lax.rsqrt(jnp.mean(x * x, axis=-1, keepdims=True) + RMS_EPS) * gain


def _chunk_cumsum(x, rin):
    s = 1
    while s < CHUNK:
        x = x + jnp.where(rin >= s, pltpu.roll(x, s, 0), 0.0)
        s *= 2
    return x


def _rmsnorm_kernel(x_ref, g_ref, o_ref):
    o_ref[...] = _rms(x_ref[...], g_ref[...]).astype(o_ref.dtype)


def _rmsnorm(x, gain, out_dtype):
    t, d = x.shape
    tm = _pick(t, (256, 128, 64, 8))
    return pl.pallas_call(
        _rmsnorm_kernel,
        grid=(t // tm,),
        in_specs=[pl.BlockSpec((tm, d), lambda i: (i, 0)),
                  pl.BlockSpec((1, d), lambda i: (0, 0))],
        out_specs=pl.BlockSpec((tm, d), lambda i: (i, 0)),
        out_shape=jax.ShapeDtypeStruct((t, d), out_dtype),
        compiler_params=_params(1),
        name="rmsnorm",
    )(x, gain.reshape(1, d).astype(F32))


def _matmul_kernel(*refs, nk, has_res):
    a_ref, b_ref = refs[0], refs[1]
    r_ref = refs[2] if has_res else None
    o_ref = refs[-1]
    part = _dot(a_ref[...], b_ref[...].astype(BF16))

    def first():
        return part if r_ref is None else r_ref[...] + part

    if nk == 1:
        o_ref[...] = first()
        return
    @pl.when(pl.program_id(2) == 0)
    def _():
        o_ref[...] = r_ref[...] if r_ref is not None else jnp.zeros_like(o_ref)

    o_ref[...] += part


def _matmul(a, b, *, tm, tn, tk, residual=None, name="matmul"):
    m, kd = a.shape
    n = b.shape[-1]
    nk = kd // tk
    in_specs = [pl.BlockSpec((tm, tk), lambda i, j, k: (i, k)),
                pl.BlockSpec((tk, tn), lambda i, j, k: (k, j))]
    args = [a, b]
    if residual is not None:
        in_specs.append(pl.BlockSpec((tm, tn), lambda i, j, k: (i, j),
                                     pipeline_mode=pl.Buffered(1)))
        args.append(residual)
    return pl.pallas_call(
        functools.partial(_matmul_kernel, nk=nk, has_res=residual is not None),
        grid=(m // tm, n // tn, nk),
        in_specs=in_specs,
        out_specs=pl.BlockSpec((tm, tn), lambda i, j, k: (i, j)),
        out_shape=jax.ShapeDtypeStruct((m, n), F32),
        compiler_params=_params(3),
        name=name,
    )(*args)


def _proj_t_kernel(*refs, shift, chunk):
    a_ref, wm_ref = refs[0], refs[1]
    wx_ref = refs[2] if shift else None
    o_ref, w_scr = refs[-2], refs[-1]
    kd, tn = w_scr.shape

    @pl.when(pl.program_id(1) == 0)
    def _():
        for r in range(kd // chunk):
            cs = slice(r * chunk, (r + 1) * chunk)
            blk = wm_ref[:, cs]
            if shift:
                cat = jnp.concatenate([blk, wx_ref[:, cs]], axis=0)
                blk = pltpu.roll(cat, tn + HEAD - shift, 0)[:tn]
            w_scr[cs, :] = blk.T.astype(BF16)

    o_ref[...] = _dot(a_ref[...], w_scr[...])


def _proj_t(a, wt, lead, *, row0, shift, width, tm, tn):
    m, kd = a.shape
    in_specs = [pl.BlockSpec((tm, kd), lambda j, i: (i, 0)),
                pl.BlockSpec((None, tn, kd), lambda j, i: (lead, row0 // tn + j, 0))]
    args = [a, wt]
    if shift:
        in_specs.append(pl.BlockSpec((None, HEAD, kd),
                                     lambda j, i: (lead, (row0 + tn * (j + 1)) // HEAD, 0)))
        args.append(wt)
    return pl.pallas_call(
        functools.partial(_proj_t_kernel, shift=shift, chunk=_pick(kd, (512, 256, 128))),
        grid=(width // tn, m // tm),
        in_specs=in_specs,
        out_specs=pl.BlockSpec((tm, tn), lambda j, i: (i, j)),
        out_shape=jax.ShapeDtypeStruct((m, width), F32),
        scratch_shapes=[pltpu.VMEM((kd, tn), BF16)],
        compiler_params=_params(2),
        name="in_proj",
    )(*args)


def _swiglu_up_kernel(a_ref, wg_ref, wu_ref, o_ref):
    a = a_ref[...]
    g = _dot(a, wg_ref[...].astype(BF16))
    u = _dot(a, wu_ref[...].astype(BF16))
    o_ref[...] = (_silu(g) * u).astype(o_ref.dtype)


def _swiglu_up(a, wg, wu, *, tm, tn):
    m, kd = a.shape
    n = wg.shape[-1]
    w_spec = pl.BlockSpec((kd, tn), lambda i, j: (0, j))
    return pl.pallas_call(
        _swiglu_up_kernel,
        grid=(m // tm, n // tn),
        in_specs=[pl.BlockSpec((tm, kd), lambda i, j: (i, 0)), w_spec, w_spec],
        out_specs=pl.BlockSpec((tm, tn), lambda i, j: (i, j)),
        out_shape=jax.ShapeDtypeStruct((m, n), BF16),
        compiler_params=_params(2),
        name="swiglu_up",
    )(a, wg, wu)


def _out_proj_kernel(*refs, n_pieces):
    a_refs = refs[:n_pieces]
    w_refs = refs[n_pieces:2 * n_pieces]
    r_ref, o_ref = refs[2 * n_pieces], refs[2 * n_pieces + 1]
    acc = r_ref[...]
    for a_ref, w_ref in zip(a_refs, w_refs):
        acc = acc + _dot(a_ref[...], w_ref[...].astype(BF16))
    o_ref[...] = acc


def _out_proj(parts, w, lead, residual, *, tm, tn):
    m = residual.shape[0]
    n = w.shape[-1]
    unit = HEAD * functools.reduce(math.gcd, [p.shape[1] // HEAD for p in parts])
    a_specs, a_args = [], []
    for p in parts:
        for c in range(p.shape[1] // unit):
            a_specs.append(pl.BlockSpec((tm, unit), lambda i, j, c=c: (i, c)))
            a_args.append(p)
    n_pieces = len(a_args)
    w_specs = [pl.BlockSpec((None, unit, tn), lambda i, j, c=c: (lead, c, j))
               for c in range(n_pieces)]
    return pl.pallas_call(
        functools.partial(_out_proj_kernel, n_pieces=n_pieces),
        grid=(m // tm, n // tn),
        in_specs=a_specs + w_specs + [pl.BlockSpec((tm, tn), lambda i, j: (i, j))],
        out_specs=pl.BlockSpec((tm, tn), lambda i, j: (i, j)),
        out_shape=jax.ShapeDtypeStruct((m, n), F32),
        compiler_params=_params(2),
        name="out_proj",
    )(*a_args, *([w] * n_pieces), residual)


def _gdn_kernel(alog_ref, dtb_ref, q_ref, qh_ref, k_ref, kh_ref, v_ref, vh_ref, z_ref, t_ref,
                wq_ref, wk_ref, wv_ref, gain_ref, o_ref, s_ref, *, tb, n_heads, group):
    hg = pl.program_id(1)
    first = pl.program_id(2) == 0

    @pl.when(first)
    def _():
        s_ref[...] = jnp.zeros_like(s_ref)

    def conv_silu(x_ref, xh_ref, w_ref):
        halo = jnp.where(first, 0.0, xh_ref[...])
        ext = jnp.concatenate([halo, x_ref[...]], axis=0)
        w = w_ref[...]
        y = pltpu.roll(ext, 3, 0)[SUB:] * w[0:1]
        y = y + pltpu.roll(ext, 2, 0)[SUB:] * w[1:2]
        y = y + pltpu.roll(ext, 1, 0)[SUB:] * w[2:3]
        y = y + ext[SUB:] * w[3:4]
        return _silu(y)

    def l2norm(x):
        return x * lax.rsqrt(jnp.sum(x * x, axis=-1, keepdims=True) + RMS_EPS)

    q_all = conv_silu(q_ref, qh_ref, wq_ref)
    k_all = conv_silu(k_ref, kh_ref, wk_ref)
    v_all = conv_silu(v_ref, vh_ref, wv_ref)
    tail = t_ref[...]
    lane = lax.broadcasted_iota(jnp.int32, (tb, HEAD), 1)
    rin = lax.broadcasted_iota(jnp.int32, (tb, HEAD), 0) & (CHUNK - 1)

    row = lax.broadcasted_iota(jnp.int32, (CHUNK, CHUNK), 0)
    col = lax.broadcasted_iota(jnp.int32, (CHUNK, CHUNK), 1)
    eye = (row == col).astype(F32)
    causal = row >= col
    strict = row > col
    level_masks = [(((row >> k) == (col >> k)) & ((row >> (k - 1)) != (col >> (k - 1)))).astype(F32)
                   for k in range(1, 7)]

    n_chunks = tb // CHUNK
    units = []
    for g in range(group):
        h = hg * group + g
        hs = slice(g * HEAD, (g + 1) * HEAD)
        q_h = l2norm(q_all[:, hs]) * (HEAD ** -0.5)
        k_h = l2norm(k_all[:, hs])
        v_h = v_all[:, hs]
        b_col = jnp.sum(jnp.where(lane == h, tail, 0.0), axis=1, keepdims=True)
        a_col = jnp.sum(jnp.where(lane == h + n_heads, tail, 0.0), axis=1, keepdims=True)
        decay_rate = jnp.exp(jnp.zeros((1, 1), F32) + alog_ref[h])
        log_decay = -decay_rate * _softplus(a_col + dtb_ref[h])
        beta_h = jnp.broadcast_to(_sigmoid(b_col), (tb, HEAD))
        gc_h = _chunk_cumsum(jnp.broadcast_to(log_decay, (tb, HEAD)), rin)
        for c in range(n_chunks):
            sl = slice(c * CHUNK, (c + 1) * CHUNK)
            units.append(dict(q=q_h[sl], k=k_h[sl], v=v_h[sl], beta=beta_h[sl], gc=gc_h[sl]))

    for p in units:
        gcol = p["gc"][:, :CHUNK]
        grow = jnp.sum(gcol * eye, axis=0, keepdims=True)
        p["decay"] = jnp.where(causal, jnp.exp(jnp.where(causal, gcol - grow, 0.0)), 0.0)
        p["egc"] = jnp.exp(p["gc"])
        p["kb"] = p["k"] * p["beta"]
    for p in units:
        p["lower"] = jnp.where(strict, _mm_nt(p["kb"], p["k"]) * p["decay"], 0.0)
        p["attn"] = _mm_nt(p["q"], p["k"]) * p["decay"]
        p["inv"] = eye - p["lower"] * level_masks[0]
    for lvl in range(1, 6):
        for p in units:
            p["inv_s"] = _split(p["inv"])
            p["step"] = _mm_split(p["inv_s"], _split(p["lower"] * level_masks[lvl]))
        for p in units:
            p["inv"] = p["inv"] - _mm_split(_split(p["step"]), p["inv_s"])
    for p in units:
        inv_s = _split(p["inv"])
        p["u"] = _mm_split(inv_s, _split(p["v"] * p["beta"]))
        p["w"] = _mm_split(inv_s, _split(p["kb"] * p["egc"]))
        g_last = p["gc"][CHUNK - 1:CHUNK, :]
        p["q_dec"] = p["q"] * p["egc"]
        p["k_dec"] = p["k"] * jnp.exp(g_last - p["gc"])
        p["e_last"] = jnp.exp(g_last)

    gain = gain_ref[...]
    states = [s_ref[g] for g in range(group)]
    for c in range(n_chunks):
        ps = [units[g * n_chunks + c] for g in range(group)]
        v_new = [p["u"] - _mm(p["w"], s) for p, s in zip(ps, states)]
        outs = [_mm(p["q_dec"], s) + _mm(p["attn"], vn) for p, s, vn in zip(ps, states, v_new)]
        states = [s * p["e_last"] + _mm_tn(p["k_dec"], vn) for p, s, vn in zip(ps, states, v_new)]
        sl = slice(c * CHUNK, (c + 1) * CHUNK)
        for g, o in enumerate(outs):
            hs = slice(g * HEAD, (g + 1) * HEAD)
            o_ref[sl, hs] = (_rms(o, gain) * _silu(z_ref[sl, hs])).astype(o_ref.dtype)
    for g in range(group):
        s_ref[g] = states[g]


def _gdn(proj, conv_w, a_log, dt_bias, gain, *, bsz, seq, n_heads, col_q, col_z, col_tail):
    group = _pick(n_heads, (4, 3, 2))
    tb = _pick(seq, (256, 128, 64))
    nb = seq // tb
    width = group * HEAD

    def main(col):
        return pl.BlockSpec((tb, width), lambda b, h, i: (b * nb + i, col // group + h))

    def halo(col):
        return pl.BlockSpec(
            (SUB, width),
            lambda b, h, i: (jnp.maximum(b * (seq // SUB) + i * (tb // SUB) - 1, 0),
                             col // group + h))

    def wspec(col):
        return pl.BlockSpec((CONV_WIDTH, width), lambda b, h, i: (0, col // group + h))

    smem = pl.BlockSpec(memory_space=pltpu.SMEM)
    col_k = col_q + n_heads
    col_v = col_q + 2 * n_heads
    assert all(c % group == 0 for c in (col_q, col_k, col_v, col_z))
    return pl.pallas_call(
        functools.partial(_gdn_kernel, tb=tb, n_heads=n_heads, group=group),
        grid=(bsz, n_heads // group, nb),
        in_specs=[smem, smem,
                  main(col_q), halo(col_q), main(col_k), halo(col_k), main(col_v), halo(col_v),
                  main(col_z),
                  pl.BlockSpec((tb, HEAD), lambda b, h, i: (b * nb + i, col_tail)),
                  wspec(0), wspec(n_heads), wspec(2 * n_heads),
                  pl.BlockSpec((1, HEAD), lambda b, h, i: (0, 0))],
        out_specs=pl.BlockSpec((tb, width), lambda b, h, i: (b * nb + i, h)),
        out_shape=jax.ShapeDtypeStruct((bsz * seq, n_heads * HEAD), BF16),
        scratch_shapes=[pltpu.VMEM((group, HEAD, HEAD), F32)],
        compiler_params=_params(3),
        name="gdn",
    )(a_log.astype(F32), dt_bias.astype(F32), proj, proj, proj, proj, proj, proj, proj, proj,
      conv_w, conv_w, conv_w, gain.reshape(1, HEAD).astype(F32))


def _hgrn_kernel(q_ref, f_ref, i_ref, g_ref, lb_ref, gain_ref, o_ref, s_ref, *, tb, layer, group):
    @pl.when(pl.program_id(2) == 0)
    def _():
        s_ref[...] = jnp.zeros_like(s_ref)

    lbp = lb_ref[...]
    e = jnp.exp(lbp - jnp.max(lbp, axis=0, keepdims=True))
    probs = e / jnp.sum(e, axis=0, keepdims=True)
    lower = jnp.sum(probs[:layer + 1], axis=0, keepdims=True) - probs[0:1]

    f_pre = f_ref[...]
    f_gate = lower + (1.0 - lower) * _sigmoid(f_pre)
    log_f = jnp.log(jnp.maximum(f_gate, MIN_GATE))
    rin = lax.broadcasted_iota(jnp.int32, f_pre.shape, 0) & (CHUNK - 1)
    b_all = _chunk_cumsum(log_f, rin)
    k_all = (1.0 - lower) * _sigmoid(-f_pre)
    q_all = _silu(q_ref[...])
    v_all = i_ref[...]

    row = lax.broadcasted_iota(jnp.int32, (CHUNK, CHUNK), 0)
    col = lax.broadcasted_iota(jnp.int32, (CHUNK, CHUNK), 1)
    sub_row = lax.broadcasted_iota(jnp.int32, (SUB, 1), 0)
    off_masks = {1 << sh: ((((row >> sh) & 1) == 1) & ((col >> sh) == (row >> sh) - 1)).astype(F32)
                 for sh in (3, 4, 5)}
    gain = gain_ref[...]
    n_sub = CHUNK // SUB
    n_chunks = tb // CHUNK

    units = []
    for g in range(group):
        hs = slice(g * HEAD, (g + 1) * HEAD)
        for c in range(n_chunks):
            sl = slice(c * CHUNK, (c + 1) * CHUNK)
            units.append(dict(q=q_all[sl, hs], k=k_all[sl, hs], b=b_all[sl, hs], v=v_all[sl, hs]))

    for p in units:
        q, k, b = p["q"], p["k"], p["b"]
        b_last = b[CHUNK - 1:CHUNK, :]
        p["q_dec"] = q * jnp.exp(b)
        p["e_last"] = jnp.exp(b_last)
        p["kv"] = _mm_tn(p["v"], k * jnp.exp(b_last - b))
    for m in (8, 16, 32):
        for p in units:
            q, k, b = p["q"], p["k"], p["b"]
            ref_q, ref_k = [], []
            for blk in range(n_sub):
                start = (blk * SUB // m) * m
                ref_q.append(jnp.broadcast_to(b[start:start + 1, :], (SUB, HEAD)))
                nxt = start + m
                if nxt < CHUNK:
                    ref_k.append(jnp.broadcast_to(b[nxt:nxt + 1, :], (SUB, HEAD)))
                else:
                    ref_k.append(b[blk * SUB:(blk + 1) * SUB, :])
            qe = q * jnp.exp(b - jnp.concatenate(ref_q, axis=0))
            ke = k * jnp.exp(jnp.concatenate(ref_k, axis=0) - b)
            part = _mm_nt(qe, ke) * off_masks[m]
            p["scores"] = part if m == 8 else p["scores"] + part
    for p in units:
        p["intra"] = _mm(p["scores"], p["v"])
    for p in units:
        q, k, b, v = p["q"], p["k"], p["b"], p["v"]
        diag = []
        for blk in range(n_sub):
            sl = slice(blk * SUB, (blk + 1) * SUB)
            qi, ki, bi, vi = q[sl], k[sl], b[sl], v[sl]
            acc = jnp.zeros((SUB, HEAD), F32)
            for s in range(SUB):
                m = sub_row >= s
                dec = jnp.where(m, jnp.exp(jnp.where(m, bi - bi[s:s + 1], 0.0)), 0.0)
                wgt = jnp.sum(qi * ki[s:s + 1] * dec, axis=1, keepdims=True)
                acc = acc + wgt * vi[s:s + 1]
            diag.append(acc)
        p["intra"] = p["intra"] + jnp.concatenate(diag, axis=0)

    for g in range(group):
        hs = slice(g * HEAD, (g + 1) * HEAD)
        state_t = s_ref[g]
        for c in range(n_chunks):
            p = units[g * n_chunks + c]
            o = p["intra"] + _mm_nt(p["q_dec"], state_t)
            state_t = state_t * p["e_last"] + p["kv"]
            sl = slice(c * CHUNK, (c + 1) * CHUNK)
            o_ref[sl, hs] = (_rms(o, gain) * _silu(g_ref[sl, hs])).astype(o_ref.dtype)
        s_ref[g] = state_t


def _hgrn(proj, hgrn_lb, gain, *, bsz, seq, n_heads, col_q, layer):
    group = _pick(n_heads, (4, 3, 2))
    tb = _pick(seq, (256, 128, 64))
    nb = seq // tb
    depth = hgrn_lb.shape[0]
    width = group * HEAD
    assert col_q % group == 0

    def main(col):
        return pl.BlockSpec((tb, width), lambda b, h, i: (b * nb + i, col // group + h))

    return pl.pallas_call(
        functools.partial(_hgrn_kernel, tb=tb, layer=layer, group=group),
        grid=(bsz, n_heads // group, nb),
        in_specs=[main(col_q), main(col_q + n_heads), main(col_q + 2 * n_heads),
                  main(col_q + 3 * n_heads),
                  pl.BlockSpec((depth, width), lambda b, h, i: (0, h)),
                  pl.BlockSpec((1, HEAD), lambda b, h, i: (0, 0))],
        out_specs=pl.BlockSpec((tb, width), lambda b, h, i: (b * nb + i, h)),
        out_shape=jax.ShapeDtypeStruct((bsz * seq, n_heads * HEAD), BF16),
        scratch_shapes=[pltpu.VMEM((group, HEAD, HEAD), F32)],
        compiler_params=_params(3),
        name="hgrn2",
    )(proj, proj, proj, proj, hgrn_lb.astype(F32), gain.reshape(1, HEAD).astype(F32))


def _dil_kernel(q0_ref, q1_ref, q2_ref, kp_ref, kc_ref, vp_ref, vc_ref,
                b0_ref, b1_ref, b2_ref, o_ref, o_scr, l_scr):
    first_super = pl.program_id(2) == 0
    q_refs = (q0_ref, q1_ref, q2_ref)
    b_refs = (b0_ref, b1_ref, b2_ref)
    qi = lax.broadcasted_iota(jnp.int32, (DIL_BLK, DIL_BLK), 0)
    ki = lax.broadcasted_iota(jnp.int32, (DIL_BLK, DIL_BLK), 1)
    valid_prev = ki >= qi
    valid_cur = ki <= qi
    scale = HEAD ** -0.5

    def rows(ref, start, dil):
        if dil == 1:
            return ref[pl.ds(start, DIL_BLK), :]
        return ref[pl.ds(start, DIL_BLK, stride=dil), :]

    for g, (window, dil) in enumerate(DIL_PAIRS):
        span = DIL_BLK * dil
        bias_p = b_refs[g][:, :DIL_BLK]
        bias_c = b_refs[g][:, DIL_BLK:]
        starts = [(s, s * span + c) for s in range(DIL_SUPER // span) for c in range(dil)]
        for u0 in range(0, len(starts), DIL_UNITS):
            units = []
            for s, start in starts[u0:u0 + DIL_UNITS]:
                before = (kp_ref, vp_ref, DIL_SUPER - span + start) if s == 0 else \
                    (kc_ref, vc_ref, start - span)
                units.append(dict(
                    s=s, start=start, q=rows(q_refs[g], start, dil),
                    k_cur=rows(kc_ref, start, dil), v_cur=rows(vc_ref, start, dil),
                    k_prev=rows(before[0], before[2], dil), v_prev=rows(before[1], before[2], dil)))
            for p in units:
                lp = jnp.where(valid_prev, _mm_nt(p["q"], p["k_prev"]) * scale + bias_p, MASK_VALUE)
                if p["s"] == 0:
                    lp = jnp.where(first_super, MASK_VALUE, lp)
                p["lp"] = lp
                p["lc"] = jnp.where(valid_cur, _mm_nt(p["q"], p["k_cur"]) * scale + bias_c,
                                    MASK_VALUE)
            for p in units:
                p["mx"] = jnp.maximum(jnp.max(p["lp"], axis=1, keepdims=True),
                                      jnp.max(p["lc"], axis=1, keepdims=True))
            for p in units:
                p["pp"] = jnp.exp(p["lp"] - p["mx"])
                p["pc"] = jnp.exp(p["lc"] - p["mx"])
                p["den"] = (jnp.sum(p["pp"], axis=1, keepdims=True)
                            + jnp.sum(p["pc"], axis=1, keepdims=True))
            for p in units:
                out = (_mm(p["pp"], p["v_prev"]) + _mm(p["pc"], p["v_cur"])) / p["den"]
                log_den = jnp.broadcast_to(p["mx"] + jnp.log(p["den"]), (DIL_BLK, HEAD))
                if dil == 1:
                    dst = pl.ds(p["start"], DIL_BLK)
                else:
                    dst = pl.ds(p["start"], DIL_BLK, stride=dil)
                o_scr[g, dst, :] = out
                l_scr[g, dst, :] = log_den

    l0, l1, l2 = l_scr[0], l_scr[1], l_scr[2]
    mx = jnp.maximum(jnp.maximum(l0, l1), l2)
    w0, w1, w2 = jnp.exp(l0 - mx), jnp.exp(l1 - mx), jnp.exp(l2 - mx)
    mixed = (w0 * o_scr[0] + w1 * o_scr[1] + w2 * o_scr[2]) / (w0 + w1 + w2)
    o_ref[...] = mixed.astype(o_ref.dtype)


def _t5_bucket(dist):
    max_exact = NUM_BUCKETS // 2
    d = jnp.maximum(dist, 1).astype(F32)
    large = max_exact + (jnp.log(d / max_exact) / math.log(MAX_DISTANCE / max_exact)
                         * (NUM_BUCKETS - max_exact)).astype(jnp.int32)
    large = jnp.clip(large, 0, NUM_BUCKETS - 1)
    return jnp.where(dist < max_exact, dist, large)


def _dil_bias_tables(rel_bias):
    n_kv = rel_bias.shape[1] // len(DIL_PAIRS)
    qi = jnp.arange(DIL_BLK)[:, None]
    ki = jnp.arange(2 * DIL_BLK)[None, :]
    steps = jnp.clip(qi + DIL_BLK - ki, 0, DIL_BLK)
    tables = []
    for g, (_, dil) in enumerate(DIL_PAIRS):
        onehot = (_t5_bucket(steps * dil)[:, :, None] == jnp.arange(NUM_BUCKETS)).astype(F32)
        heads = rel_bias.astype(F32)[:, g * n_kv:(g + 1) * n_kv]
        tables.append(jnp.einsum("qkb,bh->hqk", onehot, heads, precision=lax.Precision.HIGHEST))
    return jnp.concatenate(tables, axis=0)


def _dilated(proj, bias_tbl, *, bsz, seq, n_kv, col_q, col_k, col_v):
    ns = seq // DIL_SUPER

    def qspec(g):
        return pl.BlockSpec((DIL_SUPER, HEAD), lambda b, h, j: (b * ns + j, col_q + g * n_kv + h))

    def cur(col):
        return pl.BlockSpec((DIL_SUPER, HEAD), lambda b, h, j: (b * ns + j, col + h))

    def prev(col):
        return pl.BlockSpec((DIL_SUPER, HEAD),
                            lambda b, h, j: (b * ns + jnp.maximum(j - 1, 0), col + h))

    def bspec(g):
        return pl.BlockSpec((None, DIL_BLK, 2 * DIL_BLK), lambda b, h, j: (g * n_kv + h, 0, 0))

    return pl.pallas_call(
        _dil_kernel,
        grid=(bsz, n_kv, ns),
        in_specs=[qspec(0), qspec(1), qspec(2), prev(col_k), cur(col_k), prev(col_v), cur(col_v),
                  bspec(0), bspec(1), bspec(2)],
        out_specs=pl.BlockSpec((DIL_SUPER, HEAD), lambda b, h, j: (b * ns + j, h)),
        out_shape=jax.ShapeDtypeStruct((bsz * seq, n_kv * HEAD), BF16),
        scratch_shapes=[pltpu.VMEM((len(DIL_PAIRS), DIL_SUPER, HEAD), F32)] * 2,
        compiler_params=_params(3),
        name="dilated_attn",
    )(proj, proj, proj, proj, proj, proj, proj, bias_tbl, bias_tbl, bias_tbl)


def _router_kernel(x_ref, g_ref, w_ref, idx_ref, wgt_ref, *, n_experts):
    hn = _rms(x_ref[...], g_ref[...])
    logits = _mm_split(_split(hn), _split(w_ref[...]))
    lane = lax.broadcasted_iota(jnp.int32, logits.shape, 1)
    logits = jnp.where(lane < n_experts, logits, -jnp.inf)
    m1 = jnp.max(logits, axis=1, keepdims=True)
    i1 = jnp.min(jnp.where(logits == m1, lane, HEAD), axis=1, keepdims=True)
    rest = jnp.where(lane == i1, -jnp.inf, logits)
    m2 = jnp.max(rest, axis=1, keepdims=True)
    i2 = jnp.min(jnp.where(rest == m2, lane, HEAD), axis=1, keepdims=True)
    e2 = jnp.exp(m2 - m1)
    idx_ref[...] = jnp.where(lane == 0, i1, i2)
    wgt_ref[...] = jnp.where(lane == 0, 1.0 / (1.0 + e2), e2 / (1.0 + e2))


def _router(x, gain, w_router):
    t, d = x.shape
    n_experts = w_router.shape[1]
    tm = _pick(t, (256, 128, 64, 8))
    w_pad = jnp.zeros((d, HEAD), F32).at[:, :n_experts].set(w_router.astype(F32))
    return pl.pallas_call(
        functools.partial(_router_kernel, n_experts=n_experts),
        grid=(t // tm,),
        in_specs=[pl.BlockSpec((tm, d), lambda i: (i, 0)),
                  pl.BlockSpec((1, d), lambda i: (0, 0)),
                  pl.BlockSpec((d, HEAD), lambda i: (0, 0))],
        out_specs=[pl.BlockSpec((tm, HEAD), lambda i: (i, 0)),
                   pl.BlockSpec((tm, HEAD), lambda i: (i, 0))],
        out_shape=[jax.ShapeDtypeStruct((t, HEAD), jnp.int32),
                   jax.ShapeDtypeStruct((t, HEAD), F32)],
        compiler_params=_params(1),
        name="router",
    )(x, gain.reshape(1, d).astype(F32), w_pad)


def _routing_plan(top_idx, n_experts, tile):
    t = top_idx.shape[0]
    flat = top_idx.reshape(-1)
    onehot = (flat[:, None] == jnp.arange(n_experts)[None, :]).astype(jnp.int32)
    rank = jnp.sum((jnp.cumsum(onehot, axis=0) - onehot) * onehot, axis=1)
    counts = jnp.sum(onehot, axis=0)
    tiles_per = (counts + tile - 1) // tile
    tile_end = jnp.cumsum(tiles_per)
    dest = (tile_end - tiles_per)[flat] * tile + rank
    n_tiles = (TOP_K * t) // tile + n_experts
    tile_ids = jnp.arange(n_tiles)
    tile_expert = jnp.minimum(jnp.sum((tile_ids[:, None] >= tile_end[None, :]).astype(jnp.int32),
                                      axis=1), n_experts - 1).astype(jnp.int32)
    tile_active = (tile_ids < tile_end[-1]).astype(jnp.int32)
    src = jnp.zeros((n_tiles * tile,), jnp.int32).at[dest].set(
        jnp.arange(TOP_K * t, dtype=jnp.int32) // TOP_K)
    return src, dest.reshape(t, TOP_K).astype(jnp.int32), tile_expert, tile_active


def _row_copy(src_hbm, row, dst_vmem, slot, sem):
    return pltpu.make_async_copy(src_hbm.at[pl.ds(row, 1)], dst_vmem.at[pl.ds(slot, 1)], sem)


def _gather_norm_kernel(src_ref, act_ref, h_hbm, gain_ref, o_ref, buf, sem, *, tile):
    i = pl.program_id(0)
    last = pl.num_programs(0) - 1
    slot = i % 2

    def start_tile(t, s):
        @pl.when(act_ref[t] == 1)
        def _():
            def issue(r, carry):
                _row_copy(h_hbm, src_ref[t * tile + r], buf.at[s], r, sem.at[s]).start()
                return carry

            lax.fori_loop(0, tile, issue, 0, unroll=8)

    @pl.when(i == 0)
    def _():
        start_tile(0, 0)

    @pl.when(i < last)
    def _():
        start_tile(i + 1, 1 - slot)

    @pl.when(act_ref[i] == 1)
    def _():
        def drain(r, carry):
            _row_copy(h_hbm, 0, buf.at[slot], r, sem.at[slot]).wait()
            return carry

        lax.fori_loop(0, tile, drain, 0, unroll=8)
        o_ref[...] = _rms(buf[slot], gain_ref[...]).astype(o_ref.dtype)

    @pl.when(act_ref[i] == 0)
    def _():
        o_ref[...] = jnp.zeros_like(o_ref)


def _gather_norm(h, gain, src, tile_active, *, tile):
    d = h.shape[1]
    n_tiles = tile_active.shape[0]
    return pl.pallas_call(
        functools.partial(_gather_norm_kernel, tile=tile),
        grid_spec=pltpu.PrefetchScalarGridSpec(
            num_scalar_prefetch=2,
            grid=(n_tiles,),
            in_specs=[pl.BlockSpec(memory_space=pl.ANY),
                      pl.BlockSpec((1, d), lambda i, s, a: (0, 0))],
            out_specs=pl.BlockSpec((tile, d), lambda i, s, a: (i, 0)),
            scratch_shapes=[pltpu.VMEM((2, tile, d), F32), pltpu.SemaphoreType.DMA((2,))]),
        out_shape=jax.ShapeDtypeStruct((n_tiles * tile, d), BF16),
        compiler_params=_params(1),
        name="moe_gather",
    )(src, tile_active, h, gain.reshape(1, d).astype(F32))


def _moe_up_kernel(te_ref, act_ref, a_ref, wg_ref, wu_ref, o_ref):
    i = pl.program_id(1)

    @pl.when(act_ref[i] == 1)
    def _():
        a = a_ref[...]
        g = _dot(a, wg_ref[...].astype(BF16))
        u = _dot(a, wu_ref[...].astype(BF16))
        o_ref[...] = (_silu(g) * u).astype(o_ref.dtype)

    @pl.when(act_ref[i] == 0)
    def _():
        o_ref[...] = jnp.zeros_like(o_ref)


def _moe_up(xg, wg, wu, tile_expert, tile_active, *, tile, tn):
    d = xg.shape[1]
    n = wg.shape[-1]
    n_tiles = tile_active.shape[0]
    w_spec = pl.BlockSpec((None, d, tn), lambda j, i, te, a: (te[i], 0, j))
    return pl.pallas_call(
        _moe_up_kernel,
        grid_spec=pltpu.PrefetchScalarGridSpec(
            num_scalar_prefetch=2,
            grid=(n // tn, n_tiles),
            in_specs=[pl.BlockSpec((tile, d), lambda j, i, te, a: (i * a[i], 0)), w_spec, w_spec],
            out_specs=pl.BlockSpec((tile, tn), lambda j, i, te, a: (i, j))),
        out_shape=jax.ShapeDtypeStruct((n_tiles * tile, n), BF16),
        compiler_params=_params(2),
        name="moe_up",
    )(tile_expert, tile_active, xg, wg, wu)


def _moe_down_kernel(te_ref, act_ref, a_ref, w_ref, o_ref):
    i = pl.program_id(1)

    @pl.when(act_ref[i] == 1)
    def _():
        o_ref[...] = _dot(a_ref[...], w_ref[...].astype(BF16))

    @pl.when(act_ref[i] == 0)
    def _():
        o_ref[...] = jnp.zeros_like(o_ref)


def _moe_down(act, wd, tile_expert, tile_active, *, tile, tn):
    kd = act.shape[1]
    n = wd.shape[-1]
    n_tiles = tile_active.shape[0]
    return pl.pallas_call(
        _moe_down_kernel,
        grid_spec=pltpu.PrefetchScalarGridSpec(
            num_scalar_prefetch=2,
            grid=(n // tn, n_tiles),
            in_specs=[pl.BlockSpec((tile, kd), lambda j, i, te, a: (i * a[i], 0)),
                      pl.BlockSpec((None, kd, tn), lambda j, i, te, a: (te[i], 0, j))],
            out_specs=pl.BlockSpec((tile, tn), lambda j, i, te, a: (i, j))),
        out_shape=jax.ShapeDtypeStruct((n_tiles * tile, n), F32),
        compiler_params=_params(2),
        name="moe_down",
    )(tile_expert, tile_active, act, wd)


def _combine_kernel(d0_ref, d1_ref, y_hbm, h_ref, w_ref, g_ref, o_ref, buf0, buf1, sem, *, tc,
                    final_norm):
    i = pl.program_id(0)
    slot = i % 2

    def start_tile(t, s):
        def issue(r, carry):
            _row_copy(y_hbm, d0_ref[t * tc + r], buf0.at[s], r, sem.at[s]).start()
            _row_copy(y_hbm, d1_ref[t * tc + r], buf1.at[s], r, sem.at[s]).start()
            return carry

        lax.fori_loop(0, tc, issue, 0, unroll=8)

    @pl.when(i == 0)
    def _():
        start_tile(0, 0)

    @pl.when(i < pl.num_programs(0) - 1)
    def _():
        start_tile(i + 1, 1 - slot)

    def drain(r, carry):
        _row_copy(y_hbm, 0, buf0.at[slot], r, sem.at[slot]).wait()
        _row_copy(y_hbm, 0, buf1.at[slot], r, sem.at[slot]).wait()
        return carry

    lax.fori_loop(0, tc, drain, 0, unroll=8)
    w = w_ref[...]
    out = h_ref[...] + (w[:, 0:1] * buf0[slot] + w[:, 1:2] * buf1[slot])
    if final_norm:
        out = _rms(out, g_ref[...])
    o_ref[...] = out


def _combine(h, yg, dest, weights, final_gain):
    t, d = h.shape
    tc = _pick(t, (256, 128, 64, 8))
    final_norm = final_gain is not None
    gain = (final_gain if final_norm else jnp.ones((d,), F32)).reshape(1, d).astype(F32)
    return pl.pallas_call(
        functools.partial(_combine_kernel, tc=tc, final_norm=final_norm),
        grid_spec=pltpu.PrefetchScalarGridSpec(
            num_scalar_prefetch=2,
            grid=(t // tc,),
            in_specs=[pl.BlockSpec(memory_space=pl.ANY),
                      pl.BlockSpec((tc, d), lambda i, a, b: (i, 0)),
                      pl.BlockSpec((tc, HEAD), lambda i, a, b: (i, 0)),
                      pl.BlockSpec((1, d), lambda i, a, b: (0, 0))],
            out_specs=pl.BlockSpec((tc, d), lambda i, a, b: (i, 0)),
            scratch_shapes=[pltpu.VMEM((2, tc, d), F32), pltpu.VMEM((2, tc, d), F32),
                            pltpu.SemaphoreType.DMA((2,))]),
        out_shape=jax.ShapeDtypeStruct((t, d), F32),
        compiler_params=_params(1),
        name="moe_combine",
    )(dest[:, 0], dest[:, 1], yg, h, weights, gain)


def _moe_tile(rows):
    return 768 if rows >= 6144 else _pick(rows, (256, 128))


def _moe(h, gain, w_router, wg, wu, wd, final_gain):
    t, d = h.shape
    n_experts = w_router.shape[1]
    d_ff = wg.shape[-1]
    tile = _moe_tile(TOP_K * t)
    top_idx, top_w = _router(h, gain, w_router)
    src, dest, tile_expert, tile_active = _routing_plan(top_idx[:, :TOP_K], n_experts, tile)
    xg = _gather_norm(h, gain, src, tile_active, tile=tile)
    act = _moe_up(xg, wg, wu, tile_expert, tile_active, tile=tile, tn=_pick(d_ff, (512, 256, 128)))
    yg = _moe_down(act, wd, tile_expert, tile_active, tile=tile, tn=_pick(d, (512, 256, 128)))
    return _combine(h, yg, dest, top_w, final_gain)


def kernel(x, w_in, conv_gdn, gdn_a_log, gdn_dt_bias, gdn_norm, hgrn_lb, hgrn_norm, rel_bias,
           w_out, norm_mix, norm_ffn, w_gate_dense, w_up_dense, w_down_dense, w_router,
           w_gate_moe, w_up_moe, w_down_moe, norm_final):
    bsz, seq, d = x.shape
    depth = w_in.shape[0]
    t = bsz * seq
    n_gdn = gdn_a_log.shape[1]
    n_hgrn = hgrn_lb.shape[1] // HEAD
    n_q = rel_bias.shape[1]
    n_kv = n_q // len(DIL_PAIRS)

    split = 4 * n_gdn * HEAD
    shift = 2 * n_gdn
    rest = w_in.shape[-1] - split - shift
    tn_a = _pick(split, (512, 256, 128))
    tn_b = _pick(math.gcd(rest, split), (512, 256, 128))
    col_hq = 0
    col_dq = 4 * n_hgrn
    col_dk = col_dq + n_q
    col_dv = col_dk + n_kv

    w_in_t = jnp.swapaxes(w_in, 1, 2)
    bias_tbl = _dil_bias_tables(rel_bias)
    tm = _pick(t, (1024, 512, 256, 128))
    final_done = False

    h = x.reshape(t, d).astype(F32)
    for layer in range(depth):
        hn = _rmsnorm(h, norm_mix[layer], BF16)
        proj_a = _proj_t(hn, w_in_t, layer, row0=0, shift=0, width=split + tn_a, tm=tm, tn=tn_a)
        proj_b = _proj_t(hn, w_in_t, layer, row0=split, shift=shift, width=rest, tm=tm, tn=tn_b)
        oa = _gdn(proj_a, conv_gdn[layer].astype(F32), gdn_a_log[layer], gdn_dt_bias[layer],
                  gdn_norm[layer], bsz=bsz, seq=seq, n_heads=n_gdn, col_q=0, col_z=3 * n_gdn,
                  col_tail=split // HEAD)
        ob = _hgrn(proj_b, hgrn_lb, hgrn_norm[layer], bsz=bsz, seq=seq, n_heads=n_hgrn,
                   col_q=col_hq, layer=layer)
        oc = _dilated(proj_b, bias_tbl, bsz=bsz, seq=seq, n_kv=n_kv, col_q=col_dq, col_k=col_dk,
                      col_v=col_dv)
        h = _out_proj([oa, ob, oc], w_out, layer, h, tm=tm, tn=_pick(d, (512, 256, 128)))

        idx = layer // 2
        if layer % 2 == 0:
            hn = _rmsnorm(h, norm_ffn[layer], BF16)
            d_ff = w_gate_dense.shape[-1]
            act = _swiglu_up(hn, w_gate_dense[idx], w_up_dense[idx],
                             tm=_pick(t, (2048, 1024, 512, 256, 128)), tn=_pick(d_ff, (256, 128)))
            h = _matmul(act, w_down_dense[idx], tm=tm, tn=_pick(d, (1024, 512, 256, 128)),
                        tk=_pick(d_ff, (2048, 1792, 1024, 512, 256, 128)), residual=h,
                        name="ffn_down")
        else:
            final_done = layer == depth - 1
            h = _moe(h, norm_ffn[layer], w_router[idx], w_gate_moe[idx], w_up_moe[idx],
                     w_down_moe[idx], norm_final if final_done else None)

    out = h if final_done else _rmsnorm(h, norm_final, F32)
    return out.reshape(bsz, seq, d).astype(x.dtype)
```
